```python
import jax, jax.numpy as jnp
from jax import lax
import numpy as np

D_MODEL = 1024
BATCH = 4
SEQ = 8192
DEPTH = 2

GRID_W = 64
CTX_LEN = 256
HEAD_DIM = 64
MIX_WIDTH = D_MODEL
A_HEADS = 6
A_KV = 2
A_WINDOW = 128
A_BLOCK = 128
B_HEADS = 4
B_KV = 2
B_BLOCK = 128
C_HEADS = 6
NA_KH = 8
NA_KW = 16
D_FF = 2816
ROPE_THETA = 10000.0
EPS = 1e-6
N_MOD = 9
NEG = -1e30
KV_WIDTHS = (A_KV, A_KV, B_KV, B_KV, C_HEADS, C_HEADS)
W_IN_COLS = MIX_WIDTH + sum(KV_WIDTHS) * HEAD_DIM

kernel_name = "hybrid_dit_parallel_heads_ctx_prefix"


def _rms_norm(x, g):
    xf = x.astype(jnp.float32)
    y = xf * lax.rsqrt(jnp.mean(xf * xf, axis=-1, keepdims=True) + EPS)
    return (y * g.astype(jnp.float32)).astype(x.dtype)


def _modulate(u, shift, scale):
    return u * (1.0 + scale) + shift


def _swiglu(u, w_gate, w_up, w_down):
    return (jax.nn.silu(u @ w_gate) * (u @ w_up)) @ w_down


def _axial_rope(S):
    t = jnp.arange(S)
    row = (t // GRID_W).astype(jnp.float32)
    col = (t % GRID_W).astype(jnp.float32)
    n_freq = HEAD_DIM // 4
    inv = ROPE_THETA ** (-jnp.arange(n_freq, dtype=jnp.float32) / n_freq)
    ang = jnp.concatenate([row[:, None] * inv, col[:, None] * inv], axis=-1)
    return jnp.cos(ang), jnp.sin(ang)


def _rope(x, cos, sin):
    half = HEAD_DIM // 2
    x1, x2 = x[..., :half], x[..., half:]
    cs, sn = cos[None, :, None, :], sin[None, :, None, :]
    return jnp.concatenate([x1 * cs - x2 * sn, x2 * cs + x1 * sn], axis=-1).astype(x.dtype)


def _multi_softmax(logits):
    sizes = [int(l.shape[-1]) for l in logits]
    p = jax.nn.softmax(jnp.concatenate(logits, axis=-1), axis=-1)
    return jnp.split(p, [int(s) for s in np.cumsum(sizes)[:-1]], axis=-1)


def _heads(t, n):
    return t.reshape(*t.shape[:-1], n, HEAD_DIM)


def _split_q(p):
    a = A_HEADS * HEAD_DIM
    b = B_HEADS * HEAD_DIM
    return (_heads(p[..., :a], A_HEADS), _heads(p[..., a:a + b], B_HEADS),
            _heads(p[..., a + b:MIX_WIDTH], C_HEADS))


def _split_kv(p):
    offs = [0]
    for w in KV_WIDTHS:
        offs.append(offs[-1] + w * HEAD_DIM)
    return [_heads(p[..., offs[i]:offs[i + 1]], KV_WIDTHS[i]) for i in range(len(KV_WIDTHS))]


def _ctx_attn(qc, kc, vc, sink=None):
    Bn, L, H, d = qc.shape
    KV = kc.shape[2]
    G = H // KV
    qg = qc.reshape(Bn, L, KV, G, d)
    s = jnp.einsum('bqkgd,blkd->bkgql', qg, kc).astype(jnp.float32) * (d ** -0.5)
    if sink is None:
        p = jax.nn.softmax(s, axis=-1)
    else:
        sink_col = jnp.broadcast_to(sink.astype(jnp.float32).reshape(1, KV, G, 1, 1), (Bn, KV, G, L, 1))
        p, _ = _multi_softmax([s, sink_col])
    out = jnp.einsum('bkgql,blkd->bqkgd', p.astype(vc.dtype), vc)
    return out.reshape(Bn, L, H * d)


def _window_attn(q, k, v, kc, vc, sink):
    Bn, S, H, d = q.shape
    KV = k.shape[2]
    G = H // KV
    nb = S // A_BLOCK
    band = 3 * A_BLOCK
    scale = d ** -0.5
    pad = ((0, 0), (A_BLOCK, A_BLOCK), (0, 0), (0, 0))
    kp, vp = jnp.pad(k, pad), jnp.pad(v, pad)
    qb = jnp.moveaxis(q.reshape(Bn, nb, A_BLOCK, KV, G, d), 1, 0)
    rel = jnp.arange(band)[None, :] - A_BLOCK - jnp.arange(A_BLOCK)[:, None]
    sink_col = jnp.broadcast_to(sink.astype(jnp.float32).reshape(1, KV, G, 1, 1), (Bn, KV, G, A_BLOCK, 1))

    def block(args):
        n, q_n = args
        start = n * A_BLOCK
        k_n = lax.dynamic_slice_in_dim(kp, start, band, axis=1)
        v_n = lax.dynamic_slice_in_dim(vp, start, band, axis=1)
        kpos = start - A_BLOCK + jnp.arange(band)
        mask = (jnp.abs(rel) <= A_WINDOW) & ((kpos >= 0) & (kpos < S))[None, :]
        s_w = jnp.einsum('bqkgd,bskd->bkgqs', q_n, k_n).astype(jnp.float32) * scale
        s_w = jnp.where(mask, s_w, NEG)
        s_c = jnp.einsum('bqkgd,blkd->bkgql', q_n, kc).astype(jnp.float32) * scale
        p_w, p_c, _ = _multi_softmax([s_w, s_c, sink_col])
        return (jnp.einsum('bkgqs,bskd->bqkgd', p_w.astype(v.dtype), v_n)
                + jnp.einsum('bkgql,blkd->bqkgd', p_c.astype(vc.dtype), vc))

    out = lax.map(block, (jnp.arange(nb), qb))
    return jnp.moveaxis(out, 0, 1).reshape(Bn, S, H * d)


def _global_attn(q, k, v, kc, vc):
    Bn, S, H, d = q.shape
    KV = k.shape[2]
    G = H // KV
    nb = S // B_BLOCK
    scale = d ** -0.5
    qb = jnp.moveaxis(q.reshape(Bn, nb, B_BLOCK, KV, G, d), 1, 0)

    def block(q_n):
        s_l = jnp.einsum('bqkgd,bskd->bkgqs', q_n, k).astype(jnp.float32) * scale
        s_c = jnp.einsum('bqkgd,blkd->bkgql', q_n, kc).astype(jnp.float32) * scale
        p_l, p_c = _multi_softmax([s_l, s_c])
        return (jnp.einsum('bkgqs,bskd->bqkgd', p_l.astype(v.dtype), v)
                + jnp.einsum('bkgql,blkd->bqkgd', p_c.astype(vc.dtype), vc))

    out = lax.map(block, qb)
    return jnp.moveaxis(out, 0, 1).reshape(Bn, S, H * d)


def _neighbourhood_attn(q, k, v, kc, vc, rpb):
    Bn, S, H, d = q.shape
    rows = S // GRID_W
    kh = min(NA_KH, rows)
    kw = NA_KW
    n_keys = kh * kw
    scale = d ** -0.5
    qg = jnp.moveaxis(q.reshape(Bn, rows, GRID_W, H, d), 1, 0)
    kg = k.reshape(Bn, rows, GRID_W, H, d)
    vg = v.reshape(Bn, rows, GRID_W, H, d)
    cols = np.arange(GRID_W)
    col_idx = np.clip(cols - kw // 2, 0, GRID_W - kw)[:, None] + np.arange(kw)[None, :]
    dc = col_idx - cols[:, None] + NA_KW - 1
    rpb_col = rpb[:, :, dc]

    def gather(t, rs):
        t_rows = lax.dynamic_slice_in_dim(t, rs, kh, axis=1)
        t_nb = t_rows[:, :, col_idx]
        return jnp.transpose(t_nb, (0, 2, 1, 3, 4, 5)).reshape(Bn, GRID_W, n_keys, H, d)

    def block(args):
        r, q_r = args
        rs = jnp.clip(r - kh // 2, 0, rows - kh)
        dr = rs + jnp.arange(kh) - r + NA_KH - 1
        bias = jnp.transpose(jnp.take(rpb_col, dr, axis=1), (0, 2, 1, 3)).reshape(H, GRID_W, n_keys)
        k_nb, v_nb = gather(kg, rs), gather(vg, rs)
        s_n = jnp.einsum('bwhd,bwnhd->bhwn', q_r, k_nb).astype(jnp.float32) * scale + bias.astype(jnp.float32)
        s_c = jnp.einsum('bwhd,blhd->bhwl', q_r, kc).astype(jnp.float32) * scale
        p_n, p_c = _multi_softmax([s_n, s_c])
        return (jnp.einsum('bhwn,bwnhd->bwhd', p_n.astype(v.dtype), v_nb)
                + jnp.einsum('bhwl,blhd->bwhd', p_c.astype(vc.dtype), vc))

    out = lax.map(block, (jnp.arange(rows), qg))
    return jnp.moveaxis(out, 0, 1).reshape(Bn, S, H * d)


def setup_inputs(seed: int = 0) -> dict:
    key = jax.random.key(seed)
    ks = jax.random.split(key, 24)
    f32 = jnp.float32
    D, F = D_MODEL, D_FF

    def nrm(k, shape, s):
        return jax.random.normal(k, shape, f32) * s

    def gain(k, shape):
        return 1.0 + 0.02 * jax.random.normal(k, shape, f32)

    return {
        "x": nrm(ks[0], (BATCH, SEQ, D), 1.0),
        "c": nrm(ks[1], (BATCH, D), 1.0),
        "ctx": nrm(ks[2], (BATCH, CTX_LEN, D), 1.0),
        "c_ctx": nrm(ks[3], (D,), 1.0),
        "w_ada": nrm(ks[4], (DEPTH, D, N_MOD * D), 0.5 * D ** -0.5),
        "b_ada": nrm(ks[5], (DEPTH, N_MOD * D), 0.02),
        "norm_ffn1": gain(ks[6], (DEPTH, D)),
        "w_ffn1_gate": nrm(ks[7], (DEPTH, D, F), D ** -0.5),
        "w_ffn1_up": nrm(ks[8], (DEPTH, D, F), D ** -0.5),
        "w_ffn1_down": nrm(ks[9], (DEPTH, F, D), F ** -0.5),
        "norm_mix": gain(ks[10], (DEPTH, D)),
        "w_in": nrm(ks[11], (DEPTH, D, W_IN_COLS), D ** -0.5),
        "q_norm_glob": gain(ks[12], (DEPTH, HEAD_DIM)),
        "k_norm_glob": gain(ks[13], (DEPTH, HEAD_DIM)),
        "sink_win": nrm(ks[14], (DEPTH, A_HEADS), 0.5),
        "rpb_nbr": nrm(ks[15], (DEPTH, C_HEADS, 2 * NA_KH - 1, 2 * NA_KW - 1), 0.1),
        "w_out": nrm(ks[16], (DEPTH, MIX_WIDTH, D), MIX_WIDTH ** -0.5),
        "norm_ffn2": gain(ks[17], (DEPTH, D)),
        "w_ffn2_gate": nrm(ks[18], (DEPTH, D, F), D ** -0.5),
        "w_ffn2_up": nrm(ks[19], (DEPTH, D, F), D ** -0.5),
        "w_ffn2_down": nrm(ks[20], (DEPTH, F, D), F ** -0.5),
        "norm_final": gain(ks[21], (D,)),
    }


def reference(x, c, ctx, c_ctx, w_ada, b_ada, norm_ffn1, w_ffn1_gate, w_ffn1_up, w_ffn1_down,
              norm_mix, w_in, q_norm_glob, k_norm_glob, sink_win, rpb_nbr, w_out,
              norm_ffn2, w_ffn2_gate, w_ffn2_up, w_ffn2_down, norm_final):
    S = x.shape[1]
    cos, sin = _axial_rope(S)
    h, hc = x, ctx
    for l in range(DEPTH):
        last = l == DEPTH - 1
        mod_x = (jax.nn.silu(c) @ w_ada[l] + b_ada[l])[:, None, :]
        mod_c = jax.nn.silu(c_ctx) @ w_ada[l] + b_ada[l]
        sh1, sc1, g1, shm, scm, gm, sh2, sc2, g2 = jnp.split(mod_x, N_MOD, axis=-1)
        csh1, csc1, cg1, cshm, cscm, cgm, csh2, csc2, cg2 = jnp.split(mod_c, N_MOD, axis=-1)

        h = h + 0.5 * g1 * _swiglu(_modulate(_rms_norm(h, norm_ffn1[l]), sh1, sc1),
                                   w_ffn1_gate[l], w_ffn1_up[l], w_ffn1_down[l])
        hc = hc + 0.5 * cg1 * _swiglu(_modulate(_rms_norm(hc, norm_ffn1[l]), csh1, csc1),
                                      w_ffn1_gate[l], w_ffn1_up[l], w_ffn1_down[l])

        u = _modulate(_rms_norm(h, norm_mix[l]), shm, scm)
        uc = _modulate(_rms_norm(hc, norm_mix[l]), cshm, cscm)
        p = u @ w_in[l]
        q_w, q_g, q_n = _split_q(p[..., :MIX_WIDTH])
        k_w, v_w, k_g, v_g, k_n, v_n = _split_kv(p[..., MIX_WIDTH:])
        if last:
            kv_c = uc @ w_in[l][:, MIX_WIDTH:]
        else:
            pc = uc @ w_in[l]
            cq_w, cq_g, cq_n = _split_q(pc[..., :MIX_WIDTH])
            kv_c = pc[..., MIX_WIDTH:]
        ck_w, cv_w, ck_g, cv_g, ck_n, cv_n = _split_kv(kv_c)

        q_g = _rms_norm(q_g, q_norm_glob[l])
        k_g = _rms_norm(k_g, k_norm_glob[l])
        ck_g = _rms_norm(ck_g, k_norm_glob[l])
        q_w, k_w = _rope(q_w, cos, sin), _rope(k_w, cos, sin)
        q_g, k_g = _rope(q_g, cos, sin), _rope(k_g, cos, sin)

        y_w = _window_attn(q_w, k_w, v_w, ck_w, cv_w, sink_win[l])
        y_g = _global_attn(q_g, k_g, v_g, ck_g, cv_g)
        y_n = _neighbourhood_attn(q_n, k_n, v_n, ck_n, cv_n, rpb_nbr[l])
        h = h + gm * (jnp.concatenate([y_w, y_g, y_n], axis=-1) @ w_out[l])

        if not last:
            cq_g = _rms_norm(cq_g, q_norm_glob[l])
            yc = jnp.concatenate([_ctx_attn(cq_w, ck_w, cv_w, sink_win[l]),
                                  _ctx_attn(cq_g, ck_g, cv_g),
                                  _ctx_attn(cq_n, ck_n, cv_n)], axis=-1)
            hc = hc + cgm * (yc @ w_out[l])

        h = h + 0.5 * g2 * _swiglu(_modulate(_rms_norm(h, norm_ffn2[l]), sh2, sc2),
                                   w_ffn2_gate[l], w_ffn2_up[l], w_ffn2_down[l])
        if not last:
            hc = hc + 0.5 * cg2 * _swiglu(_modulate(_rms_norm(hc, norm_ffn2[l]), csh2, csc2),
                                          w_ffn2_gate[l], w_ffn2_up[l], w_ffn2_down[l])
    return _rms_norm(h, norm_final)
```

```python
import functools

import numpy as np
import jax
import jax.numpy as jnp
from jax import lax
from jax.experimental import pallas as pl
from jax.experimental.pallas import tpu as pltpu

D_MODEL = 1024
HEAD_DIM = 64
GRID_W = 64
A_HEADS, A_KV, A_WINDOW = 6, 2, 128
B_HEADS, B_KV = 4, 2
C_HEADS = 6
NA_KH, NA_KW = 8, 16
ROPE_THETA = 10000.0
EPS = 1e-6
N_MOD = 9
NEG = -1e30
MIX_WIDTH = D_MODEL
QK_SCALE = HEAD_DIM ** -0.5

LANES = 128
MOD_ROWS = 8
TM = 512
TQ_WIN = 512
TQ_GLOB = 512
TK_GLOB = 512
NBR_ROWS = 4
NBR_KROWS = NBR_ROWS + NA_KH - 1
VMEM_LIMIT = 56 * 1024 * 1024

F32 = jnp.float32
BF16 = jnp.bfloat16


def _cparams(n_axes):
    return pltpu.CompilerParams(dimension_semantics=("arbitrary",) * n_axes,
                                vmem_limit_bytes=VMEM_LIMIT)


def _dot(a, b):
    return jnp.dot(a, b, preferred_element_type=F32)


def _dot_nt(a, b):
    return lax.dot_general(a, b, (((1,), (1,)), ((), ())), preferred_element_type=F32)


def _rms(x, g):
    return x * lax.rsqrt(jnp.mean(x * x, axis=-1, keepdims=True) + EPS) * g


def _ada_kernel(cc_ref, w_ref, b_ref, o_ref):
    a = cc_ref[...]
    a = a * jax.nn.sigmoid(a)
    o_ref[0] = jnp.dot(a, w_ref[0], preferred_element_type=F32,
                       precision=lax.Precision.HIGHEST) + b_ref[0]


def _ada(cc, w_ada, b_ada):
    depth, d, n = w_ada.shape
    tn = 1024
    return pl.pallas_call(
        _ada_kernel,
        grid=(depth, n // tn),
        in_specs=[
            pl.BlockSpec((MOD_ROWS, d), lambda l, j: (0, 0)),
            pl.BlockSpec((1, d, tn), lambda l, j: (l, 0, j)),
            pl.BlockSpec((1, 1, tn), lambda l, j: (l, 0, j)),
        ],
        out_specs=pl.BlockSpec((1, MOD_ROWS, tn), lambda l, j: (l, 0, j)),
        out_shape=jax.ShapeDtypeStruct((depth, MOD_ROWS, n), F32),
        compiler_params=_cparams(2),
        name="ada",
    )(cc, w_ada, b_ada.reshape(depth, 1, n))


def _ffn_kernel(h_ref, mod_ref, g_ref, wg_ref, wu_ref, wd_ref, gf_ref, o_ref, *, k0, final):
    h = h_ref[...]
    mod = mod_ref[0]
    u = _rms(h, g_ref[...]) * (1.0 + mod[k0 + 1:k0 + 2]) + mod[k0:k0 + 1]
    ub = u.astype(BF16)
    gate = _dot(ub, wg_ref[...])
    up = _dot(ub, wu_ref[...])
    a = (gate * jax.nn.sigmoid(gate) * up).astype(BF16)
    y = _dot(a, wd_ref[...])
    out = h + 0.5 * mod[k0 + 2:k0 + 3] * y
    if final:
        out = _rms(out, gf_ref[...])
    o_ref[...] = out


def _row_maps(n_lat, tpb, n_batch):
    def mod_map(i):
        return (jnp.where(i < n_lat, i // tpb, n_batch), 0, 0)

    def rope_map(i):
        return (jnp.where(i < n_lat, i % tpb, tpb), 0)

    return mod_map, rope_map


def _resident(shape):
    nd = len(shape)
    return pl.BlockSpec(shape, lambda i: (0,) * nd, pipeline_mode=pl.Buffered(1))


def _ffn(h, mod_l, gain, wg, wu, wd, gain_final, *, k0, n_tiles, n_lat, tpb, n_batch, final):
    d = h.shape[1]
    f = wg.shape[1]
    mod_map, _ = _row_maps(n_lat, tpb, n_batch)
    return pl.pallas_call(
        functools.partial(_ffn_kernel, k0=k0, final=final),
        grid=(n_tiles,),
        in_specs=[
            pl.BlockSpec((TM, d), lambda i: (i, 0)),
            pl.BlockSpec((1, N_MOD, d), mod_map),
            _resident((1, d)),
            _resident((d, f)),
            _resident((d, f)),
            _resident((f, d)),
            _resident((1, d)),
        ],
        out_specs=pl.BlockSpec((TM, d), lambda i: (i, 0)),
        out_shape=jax.ShapeDtypeStruct((n_tiles * TM, d), F32),
        compiler_params=_cparams(1),
        name="ffn",
    )(h, mod_l, gain.reshape(1, d), wg, wu, wd, gain_final.reshape(1, d))


def _swap_halves(x):
    lane = lax.broadcasted_iota(jnp.int32, x.shape, 1)
    return jnp.where((lane % HEAD_DIM) < HEAD_DIM // 2,
                     pltpu.roll(x, LANES - HEAD_DIM // 2, 1),
                     pltpu.roll(x, HEAD_DIM // 2, 1))


def _head_mean_sq(x, seg):
    y = x * x
    hi = y.astype(BF16)
    lo = (y - hi.astype(F32)).astype(BF16)
    return _dot(hi, seg) + _dot(lo, seg)


def _qkv_kernel(h_ref, mod_ref, g_ref, w_ref, qg_ref, kg_ref, cos_ref, sin_ref, seg_ref,
                qw_o, qgl_o, qn_o, kw_o, kgl_o, kn_o, vw_o, vgl_o, vn_o):
    h = h_ref[...]
    mod = mod_ref[0]
    u = _rms(h, g_ref[...]) * (1.0 + mod[4:5]) + mod[3:4]
    p = _dot(u.astype(BF16), w_ref[...])
    cosf = cos_ref[...]
    sinf = sin_ref[...]
    seg = seg_ref[...]

    def emit(col, out_ref, n_heads, gain_ref, rope, scale):
        for c in range(n_heads // 2):
            xc = p[:, col + c * LANES: col + (c + 1) * LANES]
            if gain_ref is not None:
                xc = xc * lax.rsqrt(_head_mean_sq(xc, seg) + EPS) * gain_ref[...]
            if rope:
                xc = xc * cosf + _swap_halves(xc) * sinf
            if scale:
                xc = xc * QK_SCALE
            out_ref[2 * c] = xc[:, :HEAD_DIM].astype(BF16)
            out_ref[2 * c + 1] = xc[:, HEAD_DIM:].astype(BF16)

    col = 0
    emit(col, qw_o, A_HEADS, None, True, True)
    col += A_HEADS * HEAD_DIM
    emit(col, qgl_o, B_HEADS, qg_ref, True, True)
    col += B_HEADS * HEAD_DIM
    emit(col, qn_o, C_HEADS, None, False, True)
    col += C_HEADS * HEAD_DIM
    emit(col, kw_o, A_KV, None, True, False)
    col += A_KV * HEAD_DIM
    emit(col, vw_o, A_KV, None, False, False)
    col += A_KV * HEAD_DIM
    emit(col, kgl_o, B_KV, kg_ref, True, False)
    col += B_KV * HEAD_DIM
    emit(col, vgl_o, B_KV, None, False, False)
    col += B_KV * HEAD_DIM
    emit(col, kn_o, C_HEADS, None, False, False)
    col += C_HEADS * HEAD_DIM
    emit(col, vn_o, C_HEADS, None, False, False)


def _qkv(h, mod_l, gain, w_in, q_gain, k_gain, cosf, sinf, seg, *, n_tiles, n_lat, tpb, n_batch):
    r, d = h.shape
    ncol = w_in.shape[1]
    mod_map, rope_map = _row_maps(n_lat, tpb, n_batch)
    heads = (A_HEADS, B_HEADS, C_HEADS, A_KV, B_KV, C_HEADS, A_KV, B_KV, C_HEADS)
    return pl.pallas_call(
        _qkv_kernel,
        grid=(n_tiles,),
        in_specs=[
            pl.BlockSpec((TM, d), lambda i: (i, 0)),
            pl.BlockSpec((1, N_MOD, d), mod_map),
            _resident((1, d)),
            _resident((d, ncol)),
            _resident((1, LANES)),
            _resident((1, LANES)),
            pl.BlockSpec((TM, LANES), rope_map),
            pl.BlockSpec((TM, LANES), rope_map),
            _resident((LANES, LANES)),
        ],
        out_specs=[pl.BlockSpec((n, TM, HEAD_DIM), lambda i: (0, i, 0)) for n in heads],
        out_shape=[jax.ShapeDtypeStruct((n, r, HEAD_DIM), BF16) for n in heads],
        compiler_params=_cparams(1),
        name="qkv",
    )(h, mod_l, gain.reshape(1, d), w_in, jnp.tile(q_gain, 2).reshape(1, LANES),
      jnp.tile(k_gain, 2).reshape(1, LANES), cosf, sinf, seg)


def _softmax_pv(parts, extra_logit=None):
    m = functools.reduce(jnp.maximum, [jnp.max(s, axis=-1, keepdims=True) for s, _ in parts])
    if extra_logit is not None:
        m = jnp.maximum(m, extra_logit)
    den = None
    num = None
    for s, v in parts:
        e = jnp.exp(s - m)
        ds = jnp.sum(e, axis=-1, keepdims=True)
        pv = _dot(e.astype(BF16), v)
        den = ds if den is None else den + ds
        num = pv if num is None else num + pv
    if extra_logit is not None:
        den = den + jnp.exp(extra_logit - m)
    return num / den


def _win_kernel(sink_ref, q_ref, k_ref, v_ref, kc_ref, vc_ref, o_ref, *, seq):
    tq = q_ref.shape[1]
    kb = tq + 2 * A_WINDOW
    q0 = pl.program_id(1) * tq
    ks = pl.multiple_of(jnp.clip(q0 - A_WINDOW, 0, seq - kb), A_WINDOW)
    qpos = q0 + lax.broadcasted_iota(jnp.int32, (tq, kb), 0)
    kpos = ks + lax.broadcasted_iota(jnp.int32, (tq, kb), 1)
    mask = jnp.abs(qpos - kpos) <= A_WINDOW
    outs = []
    for h in range(A_HEADS):
        kv = h // (A_HEADS // A_KV)
        q = q_ref[h]
        s_w = jnp.where(mask, _dot_nt(q, k_ref[kv, pl.ds(ks, kb), :]), NEG)
        s_c = _dot_nt(q, kc_ref[kv])
        outs.append(_softmax_pv([(s_w, v_ref[kv, pl.ds(ks, kb), :]), (s_c, vc_ref[kv])],
                                extra_logit=sink_ref[0, h]))
    o_ref[...] = jnp.concatenate(outs, axis=-1).astype(o_ref.dtype)


def _attn_window(q, k, v, sink, *, n_batch, seq, ctx_len):
    r = q.shape[1]
    tq = TQ_WIN
    nq = seq // tq
    cb = n_batch * seq // ctx_len
    return pl.pallas_call(
        functools.partial(_win_kernel, seq=seq),
        grid=(n_batch, nq),
        in_specs=[
            pl.BlockSpec(memory_space=pltpu.SMEM),
            pl.BlockSpec((A_HEADS, tq, HEAD_DIM), lambda b, i: (0, b * nq + i, 0)),
            pl.BlockSpec((A_KV, seq, HEAD_DIM), lambda b, i: (0, b, 0)),
            pl.BlockSpec((A_KV, seq, HEAD_DIM), lambda b, i: (0, b, 0)),
            pl.BlockSpec((A_KV, ctx_len, HEAD_DIM), lambda b, i: (0, cb + b, 0)),
            pl.BlockSpec((A_KV, ctx_len, HEAD_DIM), lambda b, i: (0, cb + b, 0)),
        ],
        out_specs=pl.BlockSpec((tq, A_HEADS * HEAD_DIM), lambda b, i: (b * nq + i, 0)),
        out_shape=jax.ShapeDtypeStruct((r, A_HEADS * HEAD_DIM), BF16),
        compiler_params=_cparams(2),
        name="attn_window",
    )(sink.reshape(1, A_HEADS), q, k, v, k, v)


def _glob_kernel(q_ref, k_ref, v_ref, kc_ref, vc_ref, o_ref, *, seq):
    tq = q_ref.shape[1]
    grp = B_HEADS // B_KV
    outs = []
    for kv in range(B_KV):
        q = q_ref[kv * grp:(kv + 1) * grp].reshape(grp * tq, HEAD_DIM)
        s = _dot_nt(q, kc_ref[kv])
        m0 = jnp.max(s, axis=-1, keepdims=True)
        e = jnp.exp(s - m0)
        l0 = jnp.sum(e, axis=-1, keepdims=True)
        a0 = _dot(e.astype(BF16), vc_ref[kv])

        def body(c, carry, q=q, kv=kv):
            m, l, acc = carry
            off = pl.multiple_of(c * TK_GLOB, TK_GLOB)
            s = _dot_nt(q, k_ref[kv, pl.ds(off, TK_GLOB), :])
            m_new = jnp.maximum(m, jnp.max(s, axis=-1, keepdims=True))
            alpha = jnp.exp(m - m_new)
            e = jnp.exp(s - m_new)
            l = alpha * l + jnp.sum(e, axis=-1, keepdims=True)
            acc = alpha * acc + _dot(e.astype(BF16), v_ref[kv, pl.ds(off, TK_GLOB), :])
            return m_new, l, acc

        _, l, acc = lax.fori_loop(0, seq // TK_GLOB, body, (m0, l0, a0))
        o = acc / l
        outs.extend(o[g * tq:(g + 1) * tq] for g in range(grp))
    o_ref[...] = jnp.concatenate(outs, axis=-1).astype(o_ref.dtype)


def _attn_global(q, k, v, *, n_batch, seq, ctx_len):
    r = q.shape[1]
    tq = TQ_GLOB
    nq = seq // tq
    cb = n_batch * seq // ctx_len
    return pl.pallas_call(
        functools.partial(_glob_kernel, seq=seq),
        grid=(n_batch, nq),
        in_specs=[
            pl.BlockSpec((B_HEADS, tq, HEAD_DIM), lambda b, i: (0, b * nq + i, 0)),
            pl.BlockSpec((B_KV, seq, HEAD_DIM), lambda b, i: (0, b, 0)),
            pl.BlockSpec((B_KV, seq, HEAD_DIM), lambda b, i: (0, b, 0)),
            pl.BlockSpec((B_KV, ctx_len, HEAD_DIM), lambda b, i: (0, cb + b, 0)),
            pl.BlockSpec((B_KV, ctx_len, HEAD_DIM), lambda b, i: (0, cb + b, 0)),
        ],
        out_specs=pl.BlockSpec((tq, B_HEADS * HEAD_DIM), lambda b, i: (b * nq + i, 0)),
        out_shape=jax.ShapeDtypeStruct((r, B_HEADS * HEAD_DIM), BF16),
        compiler_params=_cparams(2),
        name="attn_global",
    )(q, k, v, k, v)


def _nbr_kernel(q_ref, k_ref, v_ref, kc_ref, vc_ref, bias_ref, o_ref, *, rows):
    kk = NBR_KROWS * GRID_W
    r0 = pl.program_id(2) * NBR_ROWS
    kr0 = jnp.clip(r0 - NA_KH // 2, 0, rows - NBR_KROWS)
    ks = pl.multiple_of(kr0 * GRID_W, GRID_W)
    outs = []
    for h in range(2):
        q = q_ref[h]
        s_n = _dot_nt(q, k_ref[h, pl.ds(ks, kk), :]) + bias_ref[0, h]
        s_c = _dot_nt(q, kc_ref[h])
        outs.append(_softmax_pv([(s_n, v_ref[h, pl.ds(ks, kk), :]), (s_c, vc_ref[h])]))
    o_ref[...] = jnp.concatenate(outs, axis=-1).astype(o_ref.dtype)


def _nbr_bias(rpb, rows):
    tile_r0 = (0, NBR_ROWS, rows - NBR_ROWS)
    qi = np.arange(NBR_ROWS * GRID_W)
    ki = np.arange(NBR_KROWS * GRID_W)
    qcol, kcol = qi % GRID_W, ki % GRID_W
    cs = np.clip(qcol - NA_KW // 2, 0, GRID_W - NA_KW)
    col_ok = (kcol[None, :] >= cs[:, None]) & (kcol[None, :] < cs[:, None] + NA_KW)
    dc = kcol[None, :] - qcol[:, None] + NA_KW - 1
    drs, dcs, oks = [], [], []
    for r0 in tile_r0:
        kr0 = int(np.clip(r0 - NA_KH // 2, 0, rows - NBR_KROWS))
        qrow, krow = r0 + qi // GRID_W, kr0 + ki // GRID_W
        rs = np.clip(qrow - NA_KH // 2, 0, rows - NA_KH)
        row_ok = (krow[None, :] >= rs[:, None]) & (krow[None, :] < rs[:, None] + NA_KH)
        ok = row_ok & col_ok
        dr = krow[None, :] - qrow[:, None] + NA_KH - 1
        drs.append(np.where(ok, dr, 0))
        dcs.append(np.where(ok, dc, 0))
        oks.append(ok)
    dr, dc, ok = np.stack(drs), np.stack(dcs), np.stack(oks)
    bias = jnp.where(ok[None], rpb.astype(F32)[:, dr, dc], NEG)
    return jnp.transpose(bias, (1, 0, 2, 3))


def _attn_nbr(q, k, v, bias, *, n_batch, seq, ctx_len):
    r = q.shape[1]
    rows = seq // GRID_W
    tq = NBR_ROWS * GRID_W
    kk = NBR_KROWS * GRID_W
    nq = seq // tq
    cb = n_batch * seq // ctx_len

    def bias_map(b, hp, i):
        return (jnp.where(i == 0, 0, jnp.where(i == nq - 1, 2, 1)), hp, 0, 0)

    return pl.pallas_call(
        functools.partial(_nbr_kernel, rows=rows),
        grid=(n_batch, C_HEADS // 2, nq),
        in_specs=[
            pl.BlockSpec((2, tq, HEAD_DIM), lambda b, hp, i: (hp, b * nq + i, 0)),
            pl.BlockSpec((2, seq, HEAD_DIM), lambda b, hp, i: (hp, b, 0)),
            pl.BlockSpec((2, seq, HEAD_DIM), lambda b, hp, i: (hp, b, 0)),
            pl.BlockSpec((2, ctx_len, HEAD_DIM), lambda b, hp, i: (hp, cb + b, 0)),
            pl.BlockSpec((2, ctx_len, HEAD_DIM), lambda b, hp, i: (hp, cb + b, 0)),
            pl.BlockSpec((1, 2, tq, kk), bias_map),
        ],
        out_specs=pl.BlockSpec((tq, 2 * HEAD_DIM), lambda b, hp, i: (b * nq + i, hp)),
        out_shape=jax.ShapeDtypeStruct((r, C_HEADS * HEAD_DIM), BF16),
        compiler_params=_cparams(3),
        name="attn_nbr",
    )(q, k, v, k, v, bias)


def _ctx_kernel(sink_ref, qw_ref, qg_ref, qn_ref, kw_ref, kg_ref, kn_ref, vw_ref, vg_ref, vn_ref,
                yw_in, yg_in, yn_in, yw_o, yg_o, yn_o):
    del yw_in, yg_in, yn_in

    def group(q_ref, k_ref, v_ref, o_ref, n_heads, n_kv, sink):
        outs = []
        for h in range(n_heads):
            kv = h // (n_heads // n_kv)
            s = _dot_nt(q_ref[h], k_ref[kv])
            outs.append(_softmax_pv([(s, v_ref[kv])],
                                    extra_logit=sink_ref[0, h] if sink else None))
        o_ref[...] = jnp.concatenate(outs, axis=-1).astype(o_ref.dtype)

    group(qw_ref, kw_ref, vw_ref, yw_o, A_HEADS, A_KV, True)
    group(qg_ref, kg_ref, vg_ref, yg_o, B_HEADS, B_KV, False)
    group(qn_ref, kn_ref, vn_ref, yn_o, C_HEADS, C_HEADS, False)


def _attn_ctx(sink, qs, ks, vs, ys, *, n_batch, seq, ctx_len):
    cb = n_batch * seq // ctx_len

    def hspec(n):
        return pl.BlockSpec((n, ctx_len, HEAD_DIM), lambda b: (0, cb + b, 0))

    def yspec(y):
        return pl.BlockSpec((ctx_len, y.shape[1]), lambda b: (cb + b, 0))

    any_spec = pl.BlockSpec(memory_space=pl.ANY)
    return pl.pallas_call(
        _ctx_kernel,
        grid=(n_batch,),
        in_specs=[pl.BlockSpec(memory_space=pltpu.SMEM)]
        + [hspec(a.shape[0]) for a in (*qs, *ks, *vs)] + [any_spec] * 3,
        out_specs=[yspec(y) for y in ys],
        out_shape=[jax.ShapeDtypeStruct(y.shape, y.dtype) for y in ys],
        input_output_aliases={10: 0, 11: 1, 12: 2},
        compiler_params=_cparams(1),
        name="attn_ctx",
    )(sink.reshape(1, A_HEADS), *qs, *ks, *vs, *ys)


def _out_kernel(h_ref, mod_ref, yw_ref, yg_ref, yn_ref, w_ref, o_ref):
    a, b = A_HEADS * HEAD_DIM, (A_HEADS + B_HEADS) * HEAD_DIM
    y = (_dot(yw_ref[...], w_ref[:a]) + _dot(yg_ref[...], w_ref[a:b]) + _dot(yn_ref[...], w_ref[b:]))
    o_ref[...] = h_ref[...] + mod_ref[0][5:6] * y


def _out_proj(h, mod_l, yw, yg, yn, w_out, *, n_tiles, n_lat, tpb, n_batch):
    r, d = h.shape
    mod_map, _ = _row_maps(n_lat, tpb, n_batch)
    return pl.pallas_call(
        _out_kernel,
        grid=(n_tiles,),
        in_specs=[
            pl.BlockSpec((TM, d), lambda i: (i, 0)),
            pl.BlockSpec((1, N_MOD, d), mod_map),
            pl.BlockSpec((TM, yw.shape[1]), lambda i: (i, 0)),
            pl.BlockSpec((TM, yg.shape[1]), lambda i: (i, 0)),
            pl.BlockSpec((TM, yn.shape[1]), lambda i: (i, 0)),
            _resident(w_out.shape),
        ],
        out_specs=pl.BlockSpec((TM, d), lambda i: (i, 0)),
        out_shape=jax.ShapeDtypeStruct((n_tiles * TM, d), F32),
        compiler_params=_cparams(1),
        name="out_proj",
    )(h, mod_l, yw, yg, yn, w_out)


def _rope_tables(seq):
    t = jnp.arange(seq)
    row = (t // GRID_W).astype(F32)
    col = (t % GRID_W).astype(F32)
    n_freq = HEAD_DIM // 4
    inv = ROPE_THETA ** (-jnp.arange(n_freq, dtype=F32) / n_freq)
    ang = jnp.concatenate([row[:, None] * inv, col[:, None] * inv], axis=-1)
    cos, sin = jnp.cos(ang), jnp.sin(ang)
    cosf = jnp.concatenate([jnp.tile(cos, (1, 4)), jnp.ones((TM, LANES), F32)], axis=0)
    sinf = jnp.concatenate([jnp.tile(jnp.concatenate([-sin, sin], axis=-1), (1, 2)),
                            jnp.zeros((TM, LANES), F32)], axis=0)
    return cosf, sinf


def kernel(x, c, ctx, c_ctx, w_ada, b_ada, norm_ffn1, w_ffn1_gate, w_ffn1_up, w_ffn1_down,
           norm_mix, w_in, q_norm_glob, k_norm_glob, sink_win, rpb_nbr, w_out,
           norm_ffn2, w_ffn2_gate, w_ffn2_up, w_ffn2_down, norm_final):
    n_batch, seq, d = x.shape
    ctx_len = ctx.shape[1]
    depth = w_ada.shape[0]
    r_lat, r_ctx = n_batch * seq, n_batch * ctx_len
    assert seq % TM == 0 and r_ctx % TM == 0 and n_batch + 1 <= MOD_ROWS
    assert seq % TQ_WIN == 0 and seq % TQ_GLOB == 0 and seq % TK_GLOB == 0
    assert seq % (NBR_ROWS * GRID_W) == 0 and r_lat % ctx_len == 0 and seq >= TQ_WIN + 2 * A_WINDOW
    n_lat, n_all, tpb = r_lat // TM, (r_lat + r_ctx) // TM, seq // TM
    tiles = dict(n_lat=n_lat, tpb=tpb, n_batch=n_batch)
    dims = dict(n_batch=n_batch, seq=seq, ctx_len=ctx_len)

    h = jnp.concatenate([x.reshape(r_lat, d), ctx.reshape(r_ctx, d)], axis=0)
    cc = jnp.zeros((MOD_ROWS, d), F32).at[:n_batch].set(c).at[n_batch].set(c_ctx)
    mod = _ada(cc, w_ada, b_ada).reshape(depth, MOD_ROWS, N_MOD, d)
    cosf, sinf = _rope_tables(seq)
    seg = jnp.asarray(np.kron(np.eye(2), np.full((HEAD_DIM, HEAD_DIM), 1.0 / HEAD_DIM)), BF16)

    for l in range(depth):
        last = l == depth - 1
        h = _ffn(h, mod[l], norm_ffn1[l], w_ffn1_gate[l].astype(BF16), w_ffn1_up[l].astype(BF16),
                 w_ffn1_down[l].astype(BF16), norm_final, k0=0, n_tiles=n_all, final=False, **tiles)
        qw, qg, qn, kw, kg, kn, vw, vg, vn = _qkv(
            h, mod[l], norm_mix[l], w_in[l].astype(BF16), q_norm_glob[l], k_norm_glob[l],
            cosf, sinf, seg, n_tiles=n_all, **tiles)
        yw = _attn_window(qw, kw, vw, sink_win[l], **dims)
        yg = _attn_global(qg, kg, vg, **dims)
        yn = _attn_nbr(qn, kn, vn, _nbr_bias(rpb_nbr[l], seq // GRID_W), **dims)
        if not last:
            yw, yg, yn = _attn_ctx(sink_win[l], (qw, qg, qn), (kw, kg, kn), (vw, vg, vn),
                                   (yw, yg, yn), **dims)
        n_mix = n_lat if last else n_all
        h = _out_proj(h, mod[l], yw, yg, yn, w_out[l].astype(BF16), n_tiles=n_mix, **tiles)
        h = _ffn(h, mod[l], norm_ffn2[l], w_ffn2_gate[l].astype(BF16), w_ffn2_up[l].astype(BF16),
                 w_ffn2_down[l].astype(BF16), norm_final, k0=6, n_tiles=n_mix, final=last, **tiles)
    return h.reshape(n_batch, seq, d)
```

```python
import functools

import numpy as np
import jax
import jax.numpy as jnp
from jax import lax
from jax.experimental import pallas as pl
from jax.experimental.pallas import tpu as pltpu

D_MODEL = 1024
HEAD_DIM = 64
GRID_W = 64
A_HEADS, A_KV, A_WINDOW = 6, 2, 128
B_HEADS, B_KV = 4, 2
C_HEADS = 6
NA_KH, NA_KW = 8, 16
ROPE_THETA = 10000.0
EPS = 1e-6
N_MOD = 9
NEG = -1e30
MIX_WIDTH = D_MODEL
QK_SCALE = HEAD_DIM ** -0.5
LOG2_E = 1.4426950408889634
VT_ROWS = HEAD_DIM + 16

LANES = 128
MOD_ROWS = 8
TM = 512
TQ_WIN = 512
TQ_GLOB = 512
TK_GLOB = 512
NBR_ROWS = 4
NBR_KROWS = NBR_ROWS + NA_KH - 1
VMEM_LIMIT = 56 * 1024 * 1024

F32 = jnp.float32
BF16 = jnp.bfloat16


def _cparams(n_axes):
    return pltpu.CompilerParams(dimension_semantics=("arbitrary",) * n_axes,
                                vmem_limit_bytes=VMEM_LIMIT)


def _dot(a, b):
    return jnp.dot(a, b, preferred_element_type=F32)


def _dot_nt(a, b):
    return lax.dot_general(a, b, (((1,), (1,)), ((), ())), preferred_element_type=F32)


def _rms(x, g):
    return x * lax.rsqrt(jnp.mean(x * x, axis=-1, keepdims=True) + EPS) * g


def _ada_kernel(cc_ref, w_ref, b_ref, o_ref):
    a = cc_ref[...]
    a = a * jax.nn.sigmoid(a)
    o_ref[0] = jnp.dot(a, w_ref[0], preferred_element_type=F32,
                       precision=lax.Precision.HIGHEST) + b_ref[0]


def _ada(cc, w_ada, b_ada):
    depth, d, n = w_ada.shape
    tn = 1024
    return pl.pallas_call(
        _ada_kernel,
        grid=(depth, n // tn),
        in_specs=[
            pl.BlockSpec((MOD_ROWS, d), lambda l, j: (0, 0)),
            pl.BlockSpec((1, d, tn), lambda l, j: (l, 0, j)),
            pl.BlockSpec((1, 1, tn), lambda l, j: (l, 0, j)),
        ],
        out_specs=pl.BlockSpec((1, MOD_ROWS, tn), lambda l, j: (l, 0, j)),
        out_shape=jax.ShapeDtypeStruct((depth, MOD_ROWS, n), F32),
        compiler_params=_cparams(2),
        name="ada",
    )(cc, w_ada, b_ada.reshape(depth, 1, n))


def _ffn_kernel(h_ref, mod_ref, g_ref, wg_ref, wu_ref, wd_ref, gf_ref, o_ref, *, k0, final):
    h = h_ref[...]
    mod = mod_ref[0]
    u = _rms(h, g_ref[...]) * (1.0 + mod[k0 + 1:k0 + 2]) + mod[k0:k0 + 1]
    ub = u.astype(BF16)
    gate = _dot(ub, wg_ref[...])
    up = _dot(ub, wu_ref[...])
    a = (gate * jax.nn.sigmoid(gate) * up).astype(BF16)
    y = _dot(a, wd_ref[...])
    out = h + 0.5 * mod[k0 + 2:k0 + 3] * y
    if final:
        out = _rms(out, gf_ref[...])
    o_ref[...] = out


def _row_maps(n_lat, tpb, n_batch):
    def mod_map(i):
        return (jnp.where(i < n_lat, i // tpb, n_batch), 0, 0)

    def rope_map(i):
        return (jnp.where(i < n_lat, i % tpb, tpb), 0)

    return mod_map, rope_map


def _resident(shape):
    nd = len(shape)
    return pl.BlockSpec(shape, lambda i: (0,) * nd, pipeline_mode=pl.Buffered(1))


def _ffn(h, mod_l, gain, wg, wu, wd, gain_final, *, k0, n_tiles, n_lat, tpb, n_batch, final):
    d = h.shape[1]
    f = wg.shape[1]
    mod_map, _ = _row_maps(n_lat, tpb, n_batch)
    return pl.pallas_call(
        functools.partial(_ffn_kernel, k0=k0, final=final),
        grid=(n_tiles,),
        in_specs=[
            pl.BlockSpec((TM, d), lambda i: (i, 0)),
            pl.BlockSpec((1, N_MOD, d), mod_map),
            _resident((1, d)),
            _resident((d, f)),
            _resident((d, f)),
            _resident((f, d)),
            _resident((1, d)),
        ],
        out_specs=pl.BlockSpec((TM, d), lambda i: (i, 0)),
        out_shape=jax.ShapeDtypeStruct((n_tiles * TM, d), F32),
        compiler_params=_cparams(1),
        name="ffn",
    )(h, mod_l, gain.reshape(1, d), wg, wu, wd, gain_final.reshape(1, d))


def _swap_halves(x):
    lane = lax.broadcasted_iota(jnp.int32, x.shape, 1)
    return jnp.where((lane % HEAD_DIM) < HEAD_DIM // 2,
                     pltpu.roll(x, LANES - HEAD_DIM // 2, 1),
                     pltpu.roll(x, HEAD_DIM // 2, 1))


def _head_mean_sq(x, seg):
    y = x * x
    hi = y.astype(BF16)
    lo = (y - hi.astype(F32)).astype(BF16)
    return _dot(hi, seg) + _dot(lo, seg)


def _qkv_kernel(h_ref, mod_ref, g_ref, w_ref, qg_ref, kg_ref, cos_ref, sin_ref, seg_ref,
                qw_o, qgl_o, qn_o, kw_o, kgl_o, kn_o, vw_o, vgl_o, vn_o):
    h = h_ref[...]
    mod = mod_ref[0]
    u = _rms(h, g_ref[...]) * (1.0 + mod[4:5]) + mod[3:4]
    p = _dot(u.astype(BF16), w_ref[...])
    cosf = cos_ref[...]
    sinf = sin_ref[...]
    seg = seg_ref[...]

    def emit(col, out_ref, n_heads, gain_ref, rope, scale, transpose=False):
        for c in range(n_heads // 2):
            xc = p[:, col + c * LANES: col + (c + 1) * LANES]
            if gain_ref is not None:
                xc = xc * lax.rsqrt(_head_mean_sq(xc, seg) + EPS) * gain_ref[...]
            if rope:
                xc = xc * cosf + _swap_halves(xc) * sinf
            if scale is not None:
                xc = xc * scale
            if transpose:
                xt = xc.T
                ones = jnp.ones((VT_ROWS - HEAD_DIM, xt.shape[1]), BF16)
                for j in range(2):
                    out_ref[2 * c + j, :HEAD_DIM, :] = xt[j * HEAD_DIM:(j + 1) * HEAD_DIM].astype(BF16)
                    out_ref[2 * c + j, HEAD_DIM:, :] = ones
            else:
                out_ref[2 * c] = xc[:, :HEAD_DIM].astype(BF16)
                out_ref[2 * c + 1] = xc[:, HEAD_DIM:].astype(BF16)

    col = 0
    emit(col, qw_o, A_HEADS, None, True, QK_SCALE)
    col += A_HEADS * HEAD_DIM
    emit(col, qgl_o, B_HEADS, qg_ref, True, QK_SCALE * LOG2_E)
    col += B_HEADS * HEAD_DIM
    emit(col, qn_o, C_HEADS, None, False, QK_SCALE)
    col += C_HEADS * HEAD_DIM
    emit(col, kw_o, A_KV, None, True, None)
    col += A_KV * HEAD_DIM
    emit(col, vw_o, A_KV, None, False, None)
    col += A_KV * HEAD_DIM
    emit(col, kgl_o, B_KV, kg_ref, True, None)
    col += B_KV * HEAD_DIM
    emit(col, vgl_o, B_KV, None, False, None, transpose=True)
    col += B_KV * HEAD_DIM
    emit(col, kn_o, C_HEADS, None, False, None)
    col += C_HEADS * HEAD_DIM
    emit(col, vn_o, C_HEADS, None, False, None)


def _qkv(h, mod_l, gain, w_in, q_gain, k_gain, cosf, sinf, seg, *, n_tiles, n_lat, tpb, n_batch):
    r, d = h.shape
    ncol = w_in.shape[1]
    mod_map, rope_map = _row_maps(n_lat, tpb, n_batch)
    heads = (A_HEADS, B_HEADS, C_HEADS, A_KV, B_KV, C_HEADS, A_KV, B_KV, C_HEADS)
    transposed = tuple(j == 7 for j in range(9))
    return pl.pallas_call(
        _qkv_kernel,
        grid=(n_tiles,),
        in_specs=[
            pl.BlockSpec((TM, d), lambda i: (i, 0)),
            pl.BlockSpec((1, N_MOD, d), mod_map),
            _resident((1, d)),
            _resident((d, ncol)),
            _resident((1, LANES)),
            _resident((1, LANES)),
            pl.BlockSpec((TM, LANES), rope_map),
            pl.BlockSpec((TM, LANES), rope_map),
            _resident((LANES, LANES)),
        ],
        out_specs=[pl.BlockSpec((n, VT_ROWS, TM), lambda i: (0, 0, i)) if t else
                   pl.BlockSpec((n, TM, HEAD_DIM), lambda i: (0, i, 0)) for n, t in zip(heads, transposed)],
        out_shape=[jax.ShapeDtypeStruct((n, VT_ROWS, r) if t else (n, r, HEAD_DIM), BF16)
                   for n, t in zip(heads, transposed)],
        compiler_params=_cparams(1),
        name="qkv",
    )(h, mod_l, gain.reshape(1, d), w_in, jnp.tile(q_gain, 2).reshape(1, LANES),
      jnp.tile(k_gain, 2).reshape(1, LANES), cosf, sinf, seg)


def _softmax_pv(parts, extra_logit=None, v_transposed=False, base2=False):
    m = functools.reduce(jnp.maximum, [jnp.max(s, axis=-1, keepdims=True) for s, _ in parts])
    if extra_logit is not None:
        m = jnp.maximum(m, extra_logit)
    den = None
    num = None
    for s, v in parts:
        e = jnp.exp2(s - m) if base2 else jnp.exp(s - m)
        ds = jnp.sum(e, axis=-1, keepdims=True)
        pv = _dot_nt(e.astype(BF16), v) if v_transposed else _dot(e.astype(BF16), v)
        den = ds if den is None else den + ds
        num = pv if num is None else num + pv
    if extra_logit is not None:
        den = den + jnp.exp(extra_logit - m)
    return num / den


def _win_kernel(sink_ref, q_ref, k_ref, v_ref, kc_ref, vc_ref, o_ref, *, seq):
    tq = q_ref.shape[1]
    kb = tq + 2 * A_WINDOW
    q0 = pl.program_id(1) * tq
    ks = pl.multiple_of(jnp.clip(q0 - A_WINDOW, 0, seq - kb), A_WINDOW)
    qpos = q0 + lax.broadcasted_iota(jnp.int32, (tq, kb), 0)
    kpos = ks + lax.broadcasted_iota(jnp.int32, (tq, kb), 1)
    mask = jnp.abs(qpos - kpos) <= A_WINDOW
    outs = []
    for h in range(A_HEADS):
        kv = h // (A_HEADS // A_KV)
        q = q_ref[h]
        s_w = jnp.where(mask, _dot_nt(q, k_ref[kv, pl.ds(ks, kb), :]), NEG)
        s_c = _dot_nt(q, kc_ref[kv])
        outs.append(_softmax_pv([(s_w, v_ref[kv, pl.ds(ks, kb), :]), (s_c, vc_ref[kv])],
                                extra_logit=sink_ref[0, h]))
    o_ref[...] = jnp.concatenate(outs, axis=-1).astype(o_ref.dtype)


def _attn_window(q, k, v, sink, *, n_batch, seq, ctx_len):
    r = q.shape[1]
    tq = TQ_WIN
    nq = seq // tq
    cb = n_batch * seq // ctx_len
    return pl.pallas_call(
        functools.partial(_win_kernel, seq=seq),
        grid=(n_batch, nq),
        in_specs=[
            pl.BlockSpec(memory_space=pltpu.SMEM),
            pl.BlockSpec((A_HEADS, tq, HEAD_DIM), lambda b, i: (0, b * nq + i, 0)),
            pl.BlockSpec((A_KV, seq, HEAD_DIM), lambda b, i: (0, b, 0)),
            pl.BlockSpec((A_KV, seq, HEAD_DIM), lambda b, i: (0, b, 0)),
            pl.BlockSpec((A_KV, ctx_len, HEAD_DIM), lambda b, i: (0, cb + b, 0)),
            pl.BlockSpec((A_KV, ctx_len, HEAD_DIM), lambda b, i: (0, cb + b, 0)),
        ],
        out_specs=pl.BlockSpec((tq, A_HEADS * HEAD_DIM), lambda b, i: (b * nq + i, 0)),
        out_shape=jax.ShapeDtypeStruct((r, A_HEADS * HEAD_DIM), BF16),
        compiler_params=_cparams(2),
        name="attn_window",
    )(sink.reshape(1, A_HEADS), q, k, v, k, v)


def _glob_kernel(q_ref, k_ref, vt_ref, kc_ref, vtc_ref, o_ref, s_scr, *, seq):
    tq = q_ref.shape[1]
    grp = B_HEADS // B_KV
    n_chunks = seq // TK_GLOB
    outs = []
    for kv in range(B_KV):
        q = q_ref[kv * grp:(kv + 1) * grp].reshape(grp * tq, HEAD_DIM)

        def scores(c, q=q, kv=kv):
            off = pl.multiple_of(c * TK_GLOB, TK_GLOB)
            return _dot_nt(k_ref[kv, pl.ds(off, TK_GLOB), :], q)

        def absorb(s, vt, m, acc):
            m_new = jnp.maximum(m, jnp.max(s, axis=0, keepdims=True))
            e = jnp.exp2(s - m_new).astype(BF16)
            return m_new, jnp.exp2(m - m_new) * acc + _dot(vt, e)

        def vt_chunk(c, kv=kv):
            return vt_ref[kv, :, pl.ds(pl.multiple_of(c * TK_GLOB, TK_GLOB), TK_GLOB)]

        s = _dot_nt(kc_ref[kv], q)
        m0 = jnp.max(s, axis=0, keepdims=True)
        a0 = _dot(vtc_ref[kv], jnp.exp2(s - m0).astype(BF16))
        s_scr[0] = scores(0)

        def body(j, carry):
            m, acc = carry
            s_scr[1] = scores(2 * j + 1)
            m, acc = absorb(s_scr[0], vt_chunk(2 * j), m, acc)
            s_scr[0] = scores(jnp.minimum(2 * j + 2, n_chunks - 1))
            return absorb(s_scr[1], vt_chunk(2 * j + 1), m, acc)

        _, acc = lax.fori_loop(0, n_chunks // 2, body, (m0, a0))
        o = acc[:HEAD_DIM] / acc[HEAD_DIM:HEAD_DIM + 1]
        outs.extend(o[:, g * tq:(g + 1) * tq] for g in range(grp))
    o_ref[...] = jnp.concatenate(outs, axis=0).T.astype(o_ref.dtype)


def _attn_global(q, k, vt, *, n_batch, seq, ctx_len):
    r = q.shape[1]
    tq = TQ_GLOB
    nq = seq // tq
    cb = n_batch * seq // ctx_len
    return pl.pallas_call(
        functools.partial(_glob_kernel, seq=seq),
        grid=(n_batch, nq),
        in_specs=[
            pl.BlockSpec((B_HEADS, tq, HEAD_DIM), lambda b, i: (0, b * nq + i, 0)),
            pl.BlockSpec((B_KV, seq, HEAD_DIM), lambda b, i: (0, b, 0)),
            pl.BlockSpec((B_KV, VT_ROWS, seq), lambda b, i: (0, 0, b)),
            pl.BlockSpec((B_KV, ctx_len, HEAD_DIM), lambda b, i: (0, cb + b, 0)),
            pl.BlockSpec((B_KV, VT_ROWS, ctx_len), lambda b, i: (0, 0, cb + b)),
        ],
        out_specs=pl.BlockSpec((tq, B_HEADS * HEAD_DIM), lambda b, i: (b * nq + i, 0)),
        out_shape=jax.ShapeDtypeStruct((r, B_HEADS * HEAD_DIM), BF16),
        scratch_shapes=[pltpu.VMEM((2, TK_GLOB, (B_HEADS // B_KV) * tq), F32)],
        compiler_params=_cparams(2),
        name="attn_global",
    )(q, k, vt, k, vt)


def _nbr_kernel(q_ref, k_ref, v_ref, kc_ref, vc_ref, bias_ref, o_ref, *, rows):
    kk = NBR_KROWS * GRID_W
    r0 = pl.program_id(2) * NBR_ROWS
    kr0 = jnp.clip(r0 - NA_KH // 2, 0, rows - NBR_KROWS)
    ks = pl.multiple_of(kr0 * GRID_W, GRID_W)
    outs = []
    for h in range(2):
        q = q_ref[h]
        s_n = _dot_nt(q, k_ref[h, pl.ds(ks, kk), :]) + bias_ref[0, h]
        s_c = _dot_nt(q, kc_ref[h])
        outs.append(_softmax_pv([(s_n, v_ref[h, pl.ds(ks, kk), :]), (s_c, vc_ref[h])]))
    o_ref[...] = jnp.concatenate(outs, axis=-1).astype(o_ref.dtype)


def _nbr_bias_kernel(rpb_ref, o_ref, *, rows):
    h = pl.program_id(0)
    n_dc = 2 * NA_KW - 1
    col = lax.broadcasted_iota(jnp.int32, (GRID_W, GRID_W), 0)
    kcol = lax.broadcasted_iota(jnp.int32, (GRID_W, GRID_W), 1)
    dc = kcol - col + NA_KW - 1
    cs = jnp.clip(col - NA_KW // 2, 0, GRID_W - NA_KW)
    col_ok = (kcol >= cs) & (kcol < cs + NA_KW)
    neg = jnp.full((GRID_W, GRID_W), NEG, F32)
    by_dr = []
    for dr in range(2 * NA_KH - 1):
        blk = neg
        for j in range(n_dc):
            blk = jnp.where(dc == j, rpb_ref[h, dr * n_dc + j], blk)
        by_dr.append(jnp.where(col_ok, blk, NEG))
    for cls, r0 in enumerate((0, NBR_ROWS, rows - NBR_ROWS)):
        kr0 = min(max(r0 - NA_KH // 2, 0), rows - NBR_KROWS)
        for ri in range(NBR_ROWS):
            qrow = r0 + ri
            rs = min(max(qrow - NA_KH // 2, 0), rows - NA_KH)
            blocks = [by_dr[kr0 + kri - qrow + NA_KH - 1] if rs <= kr0 + kri < rs + NA_KH else neg
                      for kri in range(NBR_KROWS)]
            o_ref[cls, 0, ri * GRID_W:(ri + 1) * GRID_W, :] = jnp.concatenate(blocks, axis=-1)


def _nbr_bias(rpb, rows):
    n_heads = rpb.shape[0]
    tq, kk = NBR_ROWS * GRID_W, NBR_KROWS * GRID_W
    return pl.pallas_call(
        functools.partial(_nbr_bias_kernel, rows=rows),
        grid=(n_heads,),
        in_specs=[pl.BlockSpec(memory_space=pltpu.SMEM)],
        out_specs=pl.BlockSpec((3, 1, tq, kk), lambda h: (0, h, 0, 0)),
        out_shape=jax.ShapeDtypeStruct((3, n_heads, tq, kk), F32),
        compiler_params=_cparams(1),
        name="nbr_bias",
    )(rpb.astype(F32).reshape(n_heads, -1))


def _attn_nbr(q, k, v, bias, *, n_batch, seq, ctx_len):
    r = q.shape[1]
    rows = seq // GRID_W
    tq = NBR_ROWS * GRID_W
    kk = NBR_KROWS * GRID_W
    nq = seq // tq
    cb = n_batch * seq // ctx_len

    def bias_map(b, hp, i):
        return (jnp.where(i == 0, 0, jnp.where(i == nq - 1, 2, 1)), hp, 0, 0)

    return pl.pallas_call(
        functools.partial(_nbr_kernel, rows=rows),
        grid=(n_batch, C_HEADS // 2, nq),
        in_specs=[
            pl.BlockSpec((2, tq, HEAD_DIM), lambda b, hp, i: (hp, b * nq + i, 0)),
            pl.BlockSpec((2, seq, HEAD_DIM), lambda b, hp, i: (hp, b, 0)),
            pl.BlockSpec((2, seq, HEAD_DIM), lambda b, hp, i: (hp, b, 0)),
            pl.BlockSpec((2, ctx_len, HEAD_DIM), lambda b, hp, i: (hp, cb + b, 0)),
            pl.BlockSpec((2, ctx_len, HEAD_DIM), lambda b, hp, i: (hp, cb + b, 0)),
            pl.BlockSpec((1, 2, tq, kk), bias_map),
        ],
        out_specs=pl.BlockSpec((tq, 2 * HEAD_DIM), lambda b, hp, i: (b * nq + i, hp)),
        out_shape=jax.ShapeDtypeStruct((r, C_HEADS * HEAD_DIM), BF16),
        compiler_params=_cparams(3),
        name="attn_nbr",
    )(q, k, v, k, v, bias)


def _ctx_kernel(sink_ref, qw_ref, qg_ref, qn_ref, kw_ref, kg_ref, kn_ref, vw_ref, vg_ref, vn_ref,
                yw_in, yg_in, yn_in, yw_o, yg_o, yn_o):
    del yw_in, yg_in, yn_in

    def group(q_ref, k_ref, v_ref, o_ref, n_heads, n_kv, sink=False, glob=False):
        outs = []
        for h in range(n_heads):
            kv = h // (n_heads // n_kv)
            s = _dot_nt(q_ref[h], k_ref[kv])
            v = v_ref[kv, :HEAD_DIM, :] if glob else v_ref[kv]
            outs.append(_softmax_pv([(s, v)], v_transposed=glob, base2=glob,
                                    extra_logit=sink_ref[0, h] if sink else None))
        o_ref[...] = jnp.concatenate(outs, axis=-1).astype(o_ref.dtype)

    group(qw_ref, kw_ref, vw_ref, yw_o, A_HEADS, A_KV, sink=True)
    group(qg_ref, kg_ref, vg_ref, yg_o, B_HEADS, B_KV, glob=True)
    group(qn_ref, kn_ref, vn_ref, yn_o, C_HEADS, C_HEADS)


def _attn_ctx(sink, qs, ks, vs, ys, *, n_batch, seq, ctx_len):
    cb = n_batch * seq // ctx_len

    def hspec(a):
        if a.shape[1] == VT_ROWS:
            return pl.BlockSpec((a.shape[0], VT_ROWS, ctx_len), lambda b: (0, 0, cb + b))
        return pl.BlockSpec((a.shape[0], ctx_len, HEAD_DIM), lambda b: (0, cb + b, 0))

    def yspec(y):
        return pl.BlockSpec((ctx_len, y.shape[1]), lambda b: (cb + b, 0))

    any_spec = pl.BlockSpec(memory_space=pl.ANY)
    return pl.pallas_call(
        _ctx_kernel,
        grid=(n_batch,),
        in_specs=[pl.BlockSpec(memory_space=pltpu.SMEM)]
        + [hspec(a) for a in (*qs, *ks, *vs)] + [any_spec] * 3,
        out_specs=[yspec(y) for y in ys],
        out_shape=[jax.ShapeDtypeStruct(y.shape, y.dtype) for y in ys],
        input_output_aliases={10: 0, 11: 1, 12: 2},
        compiler_params=_cparams(1),
        name="attn_ctx",
    )(sink.reshape(1, A_HEADS), *qs, *ks, *vs, *ys)


def _out_kernel(h_ref, mod_ref, yw_ref, yg_ref, yn_ref, w_ref, o_ref):
    a, b = A_HEADS * HEAD_DIM, (A_HEADS + B_HEADS) * HEAD_DIM
    y = (_dot(yw_ref[...], w_ref[:a]) + _dot(yg_ref[...], w_ref[a:b]) + _dot(yn_ref[...], w_ref[b:]))
    o_ref[...] = h_ref[...] + mod_ref[0][5:6] * y


def _out_proj(h, mod_l, yw, yg, yn, w_out, *, n_tiles, n_lat, tpb, n_batch):
    r, d = h.shape
    mod_map, _ = _row_maps(n_lat, tpb, n_batch)
    return pl.pallas_call(
        _out_kernel,
        grid=(n_tiles,),
        in_specs=[
            pl.BlockSpec((TM, d), lambda i: (i, 0)),
            pl.BlockSpec((1, N_MOD, d), mod_map),
            pl.BlockSpec((TM, yw.shape[1]), lambda i: (i, 0)),
            pl.BlockSpec((TM, yg.shape[1]), lambda i: (i, 0)),
            pl.BlockSpec((TM, yn.shape[1]), lambda i: (i, 0)),
            _resident(w_out.shape),
        ],
        out_specs=pl.BlockSpec((TM, d), lambda i: (i, 0)),
        out_shape=jax.ShapeDtypeStruct((n_tiles * TM, d), F32),
        compiler_params=_cparams(1),
        name="out_proj",
    )(h, mod_l, yw, yg, yn, w_out)


def _rope_tables(seq):
    t = jnp.arange(seq)
    row = (t // GRID_W).astype(F32)
    col = (t % GRID_W).astype(F32)
    n_freq = HEAD_DIM // 4
    inv = ROPE_THETA ** (-jnp.arange(n_freq, dtype=F32) / n_freq)
    ang = jnp.concatenate([row[:, None] * inv, col[:, None] * inv], axis=-1)
    cos, sin = jnp.cos(ang), jnp.sin(ang)
    cosf = jnp.concatenate([jnp.tile(cos, (1, 4)), jnp.ones((TM, LANES), F32)], axis=0)
    sinf = jnp.concatenate([jnp.tile(jnp.concatenate([-sin, sin], axis=-1), (1, 2)),
                            jnp.zeros((TM, LANES), F32)], axis=0)
    return cosf, sinf


def kernel(x, c, ctx, c_ctx, w_ada, b_ada, norm_ffn1, w_ffn1_gate, w_ffn1_up, w_ffn1_down,
           norm_mix, w_in, q_norm_glob, k_norm_glob, sink_win, rpb_nbr, w_out,
           norm_ffn2, w_ffn2_gate, w_ffn2_up, w_ffn2_down, norm_final):
    n_batch, seq, d = x.shape
    ctx_len = ctx.shape[1]
    depth = w_ada.shape[0]
    r_lat, r_ctx = n_batch * seq, n_batch * ctx_len
    assert seq % TM == 0 and r_ctx % TM == 0 and n_batch + 1 <= MOD_ROWS
    assert seq % TQ_WIN == 0 and seq % TQ_GLOB == 0 and seq % (2 * TK_GLOB) == 0
    assert seq % (NBR_ROWS * GRID_W) == 0 and r_lat % ctx_len == 0 and seq >= TQ_WIN + 2 * A_WINDOW
    n_lat, n_all, tpb = r_lat // TM, (r_lat + r_ctx) // TM, seq // TM
    tiles = dict(n_lat=n_lat, tpb=tpb, n_batch=n_batch)
    dims = dict(n_batch=n_batch, seq=seq, ctx_len=ctx_len)

    h = jnp.concatenate([x.reshape(r_lat, d), ctx.reshape(r_ctx, d)], axis=0)
    cc = jnp.zeros((MOD_ROWS, d), F32).at[:n_batch].set(c).at[n_batch].set(c_ctx)
    mod = _ada(cc, w_ada, b_ada).reshape(depth, MOD_ROWS, N_MOD, d)
    cosf, sinf = _rope_tables(seq)
    seg = jnp.asarray(np.kron(np.eye(2), np.full((HEAD_DIM, HEAD_DIM), 1.0 / HEAD_DIM)), BF16)

    for l in range(depth):
        last = l == depth - 1
        h = _ffn(h, mod[l], norm_ffn1[l], w_ffn1_gate[l].astype(BF16), w_ffn1_up[l].astype(BF16),
                 w_ffn1_down[l].astype(BF16), norm_final, k0=0, n_tiles=n_all, final=False, **tiles)
        qw, qg, qn, kw, kg, kn, vw, vg, vn = _qkv(
            h, mod[l], norm_mix[l], w_in[l].astype(BF16), q_norm_glob[l], k_norm_glob[l],
            cosf, sinf, seg, n_tiles=n_all, **tiles)
        yw = _attn_window(qw, kw, vw, sink_win[l], **dims)
        yg = _attn_global(qg, kg, vg, **dims)
        yn = _attn_nbr(qn, kn, vn, _nbr_bias(rpb_nbr[l], seq // GRID_W), **dims)
        if not last:
            yw, yg, yn = _attn_ctx(sink_win[l], (qw, qg, qn), (kw, kg, kn), (vw, vg, vn),
                                   (yw, yg, yn), **dims)
        n_mix = n_lat if last else n_all
        h = _out_proj(h, mod[l], yw, yg, yn, w_out[l].astype(BF16), n_tiles=n_mix, **tiles)
        h = _ffn(h, mod[l], norm_ffn2[l], w_ffn2_gate[l].astype(BF16), w_ffn2_up[l].astype(BF16),
                 w_ffn2_down[l].astype(BF16), norm_final, k0=6, n_tiles=n_mix, final=last, **tiles)
    return h.reshape(n_batch, seq, d)
```

```python
import functools

import numpy as np
import jax
import jax.numpy as jnp
from jax import lax
from jax.experimental import pallas as pl
from jax.experimental.pallas import tpu as pltpu

D_MODEL = 1024
HEAD_DIM = 64
GRID_W = 64
A_HEADS, A_KV, A_WINDOW = 6, 2, 128
B_HEADS, B_KV = 4, 2
C_HEADS = 6
NA_KH, NA_KW = 8, 16
ROPE_THETA = 10000.0
EPS = 1e-6
N_MOD = 9
NEG = -1e30
MIX_WIDTH = D_MODEL
QK_SCALE = HEAD_DIM ** -0.5
LOG2_E = 1.4426950408889634
VT_ROWS = HEAD_DIM + 16

LANES = 128
MOD_ROWS = 8
TM = 512
TQ_WIN = 256
WIN_TILES = 8
TQ_GLOB = 1024
TK_GLOB = 512
NBR_ROWS = 4
NBR_TILES = 8
NBR_KROWS = NBR_ROWS + NA_KH
VMEM_LIMIT = 56 * 1024 * 1024

F32 = jnp.float32
BF16 = jnp.bfloat16


def _cparams(n_axes):
    return pltpu.CompilerParams(dimension_semantics=("arbitrary",) * n_axes,
                                vmem_limit_bytes=VMEM_LIMIT)


def _dot(a, b):
    return jnp.dot(a, b, preferred_element_type=F32)


def _dot_nt(a, b):
    return lax.dot_general(a, b, (((1,), (1,)), ((), ())), preferred_element_type=F32)


def _rms(x, g):
    return x * lax.rsqrt(jnp.mean(x * x, axis=-1, keepdims=True) + EPS) * g


def _ada_kernel(cc_ref, w_ref, b_ref, o_ref):
    a = cc_ref[...]
    a = a * jax.nn.sigmoid(a)
    o_ref[0] = jnp.dot(a, w_ref[0], preferred_element_type=F32,
                       precision=lax.Precision.HIGHEST) + b_ref[0]


def _ada(cc, w_ada, b_ada):
    depth, d, n = w_ada.shape
    tn = 1024
    return pl.pallas_call(
        _ada_kernel,
        grid=(depth, n // tn),
        in_specs=[
            pl.BlockSpec((MOD_ROWS, d), lambda l, j: (0, 0)),
            pl.BlockSpec((1, d, tn), lambda l, j: (l, 0, j)),
            pl.BlockSpec((1, 1, tn), lambda l, j: (l, 0, j)),
        ],
        out_specs=pl.BlockSpec((1, MOD_ROWS, tn), lambda l, j: (l, 0, j)),
        out_shape=jax.ShapeDtypeStruct((depth, MOD_ROWS, n), F32),
        compiler_params=_cparams(2),
        name="ada",
    )(cc, w_ada, b_ada.reshape(depth, 1, n))


def _pick_tile(refs, n_lat):
    if len(refs) == 1:
        return refs[0][...]
    return jnp.where(pl.program_id(0) < n_lat, refs[0][...], refs[1][...])


def _ffn_kernel(*refs, k0, final, n_h, n_y, n_lat):
    refs = list(refs)
    h_refs = [refs.pop(0) for _ in range(n_h)]
    y_refs = [[refs.pop(0) for _ in range(n_y)] for _ in range(3 if n_y else 0)]
    mod_ref, g_ref, wg_ref, wu_ref, wd_ref, gf_ref = (refs.pop(0) for _ in range(6))
    wo_ref = refs.pop(0) if n_y else None
    (o_ref,) = refs
    h = _pick_tile(h_refs, n_lat)
    mod = mod_ref[0]
    if n_y:
        row = 0
        y = None
        for yr in y_refs:
            width = yr[0].shape[1]
            part = _dot(_pick_tile(yr, n_lat), wo_ref[row:row + width])
            y = part if y is None else y + part
            row += width
        h = h + mod[5:6] * y
    u = _rms(h, g_ref[...]) * (1.0 + mod[k0 + 1:k0 + 2]) + mod[k0:k0 + 1]
    ub = u.astype(BF16)
    gate = _dot(ub, wg_ref[...])
    up = _dot(ub, wu_ref[...])
    a = (gate * jax.nn.sigmoid(gate) * up).astype(BF16)
    y = _dot(a, wd_ref[...])
    out = h + 0.5 * mod[k0 + 2:k0 + 3] * y
    if final:
        out = _rms(out, gf_ref[...])
    o_ref[...] = out


def _row_maps(n_lat, tpb, n_batch):
    def mod_map(i):
        return (jnp.where(i < n_lat, i // tpb, n_batch), 0, 0)

    def rope_map(i):
        return (jnp.where(i < n_lat, i % tpb, tpb), 0)

    return mod_map, rope_map


def _resident(shape):
    nd = len(shape)
    return pl.BlockSpec(shape, lambda i: (0,) * nd, pipeline_mode=pl.Buffered(1))


def _tile_specs(arrays, n_lat):
    if len(arrays) == 1:
        return [pl.BlockSpec((TM, arrays[0].shape[1]), lambda i: (i, 0))]
    return [pl.BlockSpec((TM, arrays[0].shape[1]), lambda i: (jnp.minimum(i, n_lat - 1), 0)),
            pl.BlockSpec((TM, arrays[1].shape[1]), lambda i: (jnp.maximum(i - n_lat, 0), 0))]


def _ffn(h_srcs, mod_l, gain, wg, wu, wd, gain_final, *, k0, n_tiles, n_lat, tpb, n_batch, final,
         y_srcs=(), w_out=None):
    d, f = wg.shape
    mod_map, _ = _row_maps(n_lat, tpb, n_batch)
    n_y = len(y_srcs[0]) if y_srcs else 0
    streams = [h_srcs, *y_srcs]
    return pl.pallas_call(
        functools.partial(_ffn_kernel, k0=k0, final=final, n_h=len(h_srcs), n_y=n_y, n_lat=n_lat),
        grid=(n_tiles,),
        in_specs=[spec for s in streams for spec in _tile_specs(s, n_lat)] + [
            pl.BlockSpec((1, N_MOD, d), mod_map),
            _resident((1, d)),
            _resident((d, f)),
            _resident((d, f)),
            _resident((f, d)),
            _resident((1, d)),
        ] + ([_resident(w_out.shape)] if n_y else []),
        out_specs=pl.BlockSpec((TM, d), lambda i: (i, 0)),
        out_shape=jax.ShapeDtypeStruct((n_tiles * TM, d), F32),
        compiler_params=_cparams(1),
        name="ffn",
    )(*[a for s in streams for a in s], mod_l, gain.reshape(1, d), wg, wu, wd, gain_final.reshape(1, d),
      *([w_out] if n_y else []))


def _swap_halves(x):
    lane = lax.broadcasted_iota(jnp.int32, x.shape, 1)
    return jnp.where((lane % HEAD_DIM) < HEAD_DIM // 2,
                     pltpu.roll(x, LANES - HEAD_DIM // 2, 1),
                     pltpu.roll(x, HEAD_DIM // 2, 1))


def _head_mean_sq(x, seg):
    y = x * x
    hi = y.astype(BF16)
    lo = (y - hi.astype(F32)).astype(BF16)
    return _dot(hi, seg) + _dot(lo, seg)


def _qkv_kernel(h_ref, mod_ref, g_ref, w_ref, qg_ref, kg_ref, cos_ref, sin_ref, seg_ref,
                qw_o, qgl_o, qn_o, kw_o, kgl_o, kn_o, vw_o, vgl_o, vn_o):
    h = h_ref[...]
    mod = mod_ref[0]
    u = _rms(h, g_ref[...]) * (1.0 + mod[4:5]) + mod[3:4]
    p = _dot(u.astype(BF16), w_ref[...])
    cosf = cos_ref[...]
    sinf = sin_ref[...]
    seg = seg_ref[...]

    def emit(col, out_ref, n_heads, gain_ref, rope, scale, transpose=False):
        for c in range(n_heads // 2):
            xc = p[:, col + c * LANES: col + (c + 1) * LANES]
            if gain_ref is not None:
                xc = xc * lax.rsqrt(_head_mean_sq(xc, seg) + EPS) * gain_ref[...]
            if rope:
                xc = xc * cosf + _swap_halves(xc) * sinf
            if scale is not None:
                xc = xc * scale
            if transpose:
                xt = xc.T
                ones = jnp.ones((VT_ROWS - HEAD_DIM, xt.shape[1]), BF16)
                for j in range(2):
                    out_ref[2 * c + j, :HEAD_DIM, :] = xt[j * HEAD_DIM:(j + 1) * HEAD_DIM].astype(BF16)
                    out_ref[2 * c + j, HEAD_DIM:, :] = ones
            else:
                out_ref[2 * c] = xc[:, :HEAD_DIM].astype(BF16)
                out_ref[2 * c + 1] = xc[:, HEAD_DIM:].astype(BF16)

    col = 0
    q_scale = QK_SCALE * LOG2_E
    emit(col, qw_o, A_HEADS, None, True, q_scale)
    col += A_HEADS * HEAD_DIM
    emit(col, qgl_o, B_HEADS, qg_ref, True, q_scale)
    col += B_HEADS * HEAD_DIM
    emit(col, qn_o, C_HEADS, None, False, q_scale)
    col += C_HEADS * HEAD_DIM
    emit(col, kw_o, A_KV, None, True, None)
    col += A_KV * HEAD_DIM
    emit(col, vw_o, A_KV, None, False, None, transpose=True)
    col += A_KV * HEAD_DIM
    emit(col, kgl_o, B_KV, kg_ref, True, None)
    col += B_KV * HEAD_DIM
    emit(col, vgl_o, B_KV, None, False, None, transpose=True)
    col += B_KV * HEAD_DIM
    emit(col, kn_o, C_HEADS, None, False, None)
    col += C_HEADS * HEAD_DIM
    emit(col, vn_o, C_HEADS, None, False, None, transpose=True)


def _qkv(h, mod_l, gain, w_in, q_gain, k_gain, cosf, sinf, seg, *, n_tiles, n_lat, tpb, n_batch):
    r, d = h.shape
    ncol = w_in.shape[1]
    mod_map, rope_map = _row_maps(n_lat, tpb, n_batch)
    heads = (A_HEADS, B_HEADS, C_HEADS, A_KV, B_KV, C_HEADS, A_KV, B_KV, C_HEADS)
    transposed = tuple(j >= 6 for j in range(9))
    return pl.pallas_call(
        _qkv_kernel,
        grid=(n_tiles,),
        in_specs=[
            pl.BlockSpec((TM, d), lambda i: (i, 0)),
            pl.BlockSpec((1, N_MOD, d), mod_map),
            _resident((1, d)),
            _resident((d, ncol)),
            _resident((1, LANES)),
            _resident((1, LANES)),
            pl.BlockSpec((TM, LANES), rope_map),
            pl.BlockSpec((TM, LANES), rope_map),
            _resident((LANES, LANES)),
        ],
        out_specs=[pl.BlockSpec((n, VT_ROWS, TM), lambda i: (0, 0, i)) if t else
                   pl.BlockSpec((n, TM, HEAD_DIM), lambda i: (0, i, 0)) for n, t in zip(heads, transposed)],
        out_shape=[jax.ShapeDtypeStruct((n, VT_ROWS, r) if t else (n, r, HEAD_DIM), BF16)
                   for n, t in zip(heads, transposed)],
        compiler_params=_cparams(1),
        name="qkv",
    )(h, mod_l, gain.reshape(1, d), w_in, jnp.tile(q_gain, 2).reshape(1, LANES),
      jnp.tile(k_gain, 2).reshape(1, LANES), cosf, sinf, seg)


def _softmax_pv_t(parts, extra_logit=None):
    m = functools.reduce(jnp.maximum, [jnp.max(s, axis=0, keepdims=True) for s, _ in parts])
    if extra_logit is not None:
        m = jnp.maximum(m, extra_logit)
    acc = None
    for s, vt in parts:
        pv = _dot(vt, jnp.exp2(s - m).astype(BF16))
        acc = pv if acc is None else acc + pv
    den = acc[HEAD_DIM:HEAD_DIM + 1]
    if extra_logit is not None:
        den = den + jnp.exp2(extra_logit - m)
    return acc[:HEAD_DIM] / den


def _win_kernel(sink_ref, q_ref, k_ref, vt_ref, kc_ref, vtc_ref, o_ref, s_scr, mask_scr, *, seq):
    tq = TQ_WIN
    n_t = q_ref.shape[1] // tq
    grp = A_HEADS // A_KV
    kb = tq + 2 * A_WINDOW
    tile0 = pl.program_id(1) * n_t
    last_tile = seq // tq - 1
    sinks = [jnp.concatenate([jnp.full((1, tq), sink_ref[0, kv * grp + g] * LOG2_E, F32)
                              for g in range(grp)], axis=1) for kv in range(A_KV)]

    @pl.when((pl.program_id(0) == 0) & (pl.program_id(1) == 0))
    def _():
        rel = (lax.broadcasted_iota(jnp.int32, (kb, tq), 0) - lax.broadcasted_iota(jnp.int32, (kb, tq), 1))
        for cls in range(3):
            ok = jnp.abs(rel - cls * A_WINDOW) <= A_WINDOW
            mask_scr[cls] = jnp.concatenate([jnp.where(ok, 0.0, NEG)] * grp, axis=1)

    def band(t):
        g = tile0 + t
        q0 = g * tq
        cls = jnp.where(g == 0, 0, jnp.where(g == last_tile, 2, 1))
        return cls, pl.multiple_of(jnp.clip(q0 - A_WINDOW, 0, seq - kb), A_WINDOW)

    def scores(slot, t, kv):
        cls, ks = band(t)
        q = q_ref[kv * grp:(kv + 1) * grp, pl.ds(pl.multiple_of(t * tq, tq), tq), :]
        q = q.reshape(grp * tq, HEAD_DIM)
        s_scr[slot, :kb, :] = _dot_nt(k_ref[kv, pl.ds(ks, kb), :], q) + mask_scr[cls]
        s_scr[slot, kb:, :] = _dot_nt(kc_ref[kv], q)

    def absorb(slot, t, kv):
        _, ks = band(t)
        s = s_scr[slot]
        m = jnp.maximum(jnp.max(s, axis=0, keepdims=True), sinks[kv])
        e = jnp.exp2(s - m).astype(BF16)
        acc = _dot(vt_ref[kv, :, pl.ds(ks, kb)], e[:kb]) + _dot(vtc_ref[kv], e[kb:])
        o = acc[:HEAD_DIM] / (acc[HEAD_DIM:HEAD_DIM + 1] + jnp.exp2(sinks[kv] - m))
        return [o[:, g * tq:(g + 1) * tq] for g in range(grp)]

    scores(0, 0, 0)

    def body(t, carry):
        scores(1, t, 1)
        outs = absorb(0, t, 0)
        scores(0, jnp.minimum(t + 1, n_t - 1), 0)
        outs += absorb(1, t, 1)
        o_ref[pl.ds(pl.multiple_of(t * tq, tq), tq), :] = jnp.concatenate(outs, axis=0).T.astype(o_ref.dtype)
        return carry

    lax.fori_loop(0, n_t, body, 0)


def _attn_window(q, k, vt, sink, *, n_batch, seq, ctx_len):
    r = n_batch * seq
    n_t = min(WIN_TILES, seq // TQ_WIN)
    tb = n_t * TQ_WIN
    nq = seq // tb
    cb = n_batch * seq // ctx_len
    return pl.pallas_call(
        functools.partial(_win_kernel, seq=seq),
        grid=(n_batch, nq),
        in_specs=[
            pl.BlockSpec(memory_space=pltpu.SMEM),
            pl.BlockSpec((A_HEADS, tb, HEAD_DIM), lambda b, i: (0, b * nq + i, 0)),
            pl.BlockSpec((A_KV, seq, HEAD_DIM), lambda b, i: (0, b, 0)),
            pl.BlockSpec((A_KV, VT_ROWS, seq), lambda b, i: (0, 0, b)),
            pl.BlockSpec((A_KV, ctx_len, HEAD_DIM), lambda b, i: (0, cb + b, 0)),
            pl.BlockSpec((A_KV, VT_ROWS, ctx_len), lambda b, i: (0, 0, cb + b)),
        ],
        out_specs=pl.BlockSpec((tb, A_HEADS * HEAD_DIM), lambda b, i: (b * nq + i, 0)),
        out_shape=jax.ShapeDtypeStruct((r, A_HEADS * HEAD_DIM), BF16),
        scratch_shapes=[pltpu.VMEM((2, TQ_WIN + 2 * A_WINDOW + ctx_len, (A_HEADS // A_KV) * TQ_WIN), F32),
                        pltpu.VMEM((3, TQ_WIN + 2 * A_WINDOW, (A_HEADS // A_KV) * TQ_WIN), F32)],
        compiler_params=_cparams(2),
        name="attn_window",
    )(sink.reshape(1, A_HEADS), q, k, vt, k, vt)


def _glob_kernel(q_ref, k_ref, vt_ref, kc_ref, vtc_ref, o_ref, s_scr, *, seq):
    tq = q_ref.shape[1]
    grp = B_HEADS // B_KV
    n_chunks = seq // TK_GLOB
    outs = []
    for kv in range(B_KV):
        q = q_ref[kv * grp:(kv + 1) * grp].reshape(grp * tq, HEAD_DIM)

        def scores(c, q=q, kv=kv):
            off = pl.multiple_of(c * TK_GLOB, TK_GLOB)
            return _dot_nt(k_ref[kv, pl.ds(off, TK_GLOB), :], q)

        def absorb(s, vt, m, acc):
            m_new = jnp.maximum(m, jnp.max(s, axis=0, keepdims=True))
            e = jnp.exp2(s - m_new).astype(BF16)
            return m_new, jnp.exp2(m - m_new) * acc + _dot(vt, e)

        def vt_chunk(c, kv=kv):
            return vt_ref[kv, :, pl.ds(pl.multiple_of(c * TK_GLOB, TK_GLOB), TK_GLOB)]

        s = _dot_nt(kc_ref[kv], q)
        m0 = jnp.max(s, axis=0, keepdims=True)
        a0 = _dot(vtc_ref[kv], jnp.exp2(s - m0).astype(BF16))
        s_scr[0] = scores(0)

        def body(j, carry):
            m, acc = carry
            s_scr[1] = scores(2 * j + 1)
            m, acc = absorb(s_scr[0], vt_chunk(2 * j), m, acc)
            s_scr[0] = scores(jnp.minimum(2 * j + 2, n_chunks - 1))
            return absorb(s_scr[1], vt_chunk(2 * j + 1), m, acc)

        _, acc = lax.fori_loop(0, n_chunks // 2, body, (m0, a0))
        o = acc[:HEAD_DIM] / acc[HEAD_DIM:HEAD_DIM + 1]
        outs.extend(o[:, g * tq:(g + 1) * tq] for g in range(grp))
    o_ref[...] = jnp.concatenate(outs, axis=0).T.astype(o_ref.dtype)


def _attn_global(q, k, vt, *, n_batch, seq, ctx_len):
    r = n_batch * seq
    tq = TQ_GLOB
    nq = seq // tq
    cb = n_batch * seq // ctx_len
    return pl.pallas_call(
        functools.partial(_glob_kernel, seq=seq),
        grid=(n_batch, nq),
        in_specs=[
            pl.BlockSpec((B_HEADS, tq, HEAD_DIM), lambda b, i: (0, b * nq + i, 0)),
            pl.BlockSpec((B_KV, seq, HEAD_DIM), lambda b, i: (0, b, 0)),
            pl.BlockSpec((B_KV, VT_ROWS, seq), lambda b, i: (0, 0, b)),
            pl.BlockSpec((B_KV, ctx_len, HEAD_DIM), lambda b, i: (0, cb + b, 0)),
            pl.BlockSpec((B_KV, VT_ROWS, ctx_len), lambda b, i: (0, 0, cb + b)),
        ],
        out_specs=pl.BlockSpec((tq, B_HEADS * HEAD_DIM), lambda b, i: (b * nq + i, 0)),
        out_shape=jax.ShapeDtypeStruct((r, B_HEADS * HEAD_DIM), BF16),
        scratch_shapes=[pltpu.VMEM((2, TK_GLOB, (B_HEADS // B_KV) * tq), F32)],
        compiler_params=_cparams(2),
        name="attn_global",
    )(q, k, vt, k, vt)


def _nbr_kernel(q_ref, k_ref, vt_ref, kc_ref, vtc_ref, bias_ref, o_ref, s_scr, *, rows):
    tq = NBR_ROWS * GRID_W
    kk = NBR_KROWS * GRID_W
    n_t = q_ref.shape[1] // tq
    tile0 = pl.program_id(2) * n_t
    last_tile = rows // NBR_ROWS - 1

    def band(t):
        g = tile0 + t
        cls = jnp.where(g == 0, 0, jnp.where(g == last_tile, 2, 1))
        kr0 = jnp.clip(g * NBR_ROWS - NA_KH // 2, 0, rows - NBR_KROWS)
        return cls, pl.multiple_of(kr0 * GRID_W, LANES)

    def scores(slot, t, h):
        cls, ks = band(t)
        q = q_ref[h, pl.ds(pl.multiple_of(t * tq, tq), tq), :]
        s_scr[slot, :kk, :] = _dot_nt(k_ref[h, pl.ds(ks, kk), :], q) + bias_ref[cls, h]
        s_scr[slot, kk:, :] = _dot_nt(kc_ref[h], q)

    def absorb(slot, t, h):
        _, ks = band(t)
        s = s_scr[slot]
        e = jnp.exp2(s - jnp.max(s, axis=0, keepdims=True)).astype(BF16)
        acc = _dot(vt_ref[h, :, pl.ds(ks, kk)], e[:kk]) + _dot(vtc_ref[h], e[kk:])
        return acc[:HEAD_DIM] / acc[HEAD_DIM:HEAD_DIM + 1]

    scores(0, 0, 0)

    def body(t, carry):
        scores(1, t, 1)
        o0 = absorb(0, t, 0)
        scores(0, jnp.minimum(t + 1, n_t - 1), 0)
        o1 = absorb(1, t, 1)
        o_ref[pl.ds(pl.multiple_of(t * tq, tq), tq), :] = jnp.concatenate([o0, o1], axis=0).T.astype(o_ref.dtype)
        return carry

    lax.fori_loop(0, n_t, body, 0)


def _nbr_bias_kernel(rpb_ref, o_ref, *, rows):
    h = pl.program_id(0)
    n_dc = 2 * NA_KW - 1
    kcol = lax.broadcasted_iota(jnp.int32, (GRID_W, GRID_W), 0)
    col = lax.broadcasted_iota(jnp.int32, (GRID_W, GRID_W), 1)
    dc = kcol - col + NA_KW - 1
    cs = jnp.clip(col - NA_KW // 2, 0, GRID_W - NA_KW)
    col_ok = (kcol >= cs) & (kcol < cs + NA_KW)
    neg = jnp.full((GRID_W, GRID_W), NEG, F32)
    by_dr = []
    for dr in range(2 * NA_KH - 1):
        blk = neg
        for j in range(n_dc):
            blk = jnp.where(dc == j, rpb_ref[h, dr * n_dc + j] * LOG2_E, blk)
        by_dr.append(jnp.where(col_ok, blk, NEG))
    for cls, r0 in enumerate((0, NBR_ROWS, rows - NBR_ROWS)):
        kr0 = min(max(r0 - NA_KH // 2, 0), rows - NBR_KROWS)
        for kri in range(NBR_KROWS):
            krow = kr0 + kri
            blocks = []
            for ri in range(NBR_ROWS):
                rs = min(max(r0 + ri - NA_KH // 2, 0), rows - NA_KH)
                blocks.append(by_dr[krow - r0 - ri + NA_KH - 1] if rs <= krow < rs + NA_KH else neg)
            o_ref[cls, 0, kri * GRID_W:(kri + 1) * GRID_W, :] = jnp.concatenate(blocks, axis=-1)


def _nbr_bias(rpb, rows):
    n_heads = rpb.shape[0]
    tq, kk = NBR_ROWS * GRID_W, NBR_KROWS * GRID_W
    return pl.pallas_call(
        functools.partial(_nbr_bias_kernel, rows=rows),
        grid=(n_heads,),
        in_specs=[pl.BlockSpec(memory_space=pltpu.SMEM)],
        out_specs=pl.BlockSpec((3, 1, kk, tq), lambda h: (0, h, 0, 0)),
        out_shape=jax.ShapeDtypeStruct((3, n_heads, kk, tq), F32),
        compiler_params=_cparams(1),
        name="nbr_bias",
    )(rpb.astype(F32).reshape(n_heads, -1))


def _attn_nbr(q, k, vt, bias, *, n_batch, seq, ctx_len):
    r = n_batch * seq
    rows = seq // GRID_W
    tq = NBR_ROWS * GRID_W
    kk = NBR_KROWS * GRID_W
    tb = min(NBR_TILES, seq // tq) * tq
    nq = seq // tb
    cb = n_batch * seq // ctx_len
    return pl.pallas_call(
        functools.partial(_nbr_kernel, rows=rows),
        grid=(n_batch, C_HEADS // 2, nq),
        in_specs=[
            pl.BlockSpec((2, tb, HEAD_DIM), lambda b, hp, i: (hp, b * nq + i, 0)),
            pl.BlockSpec((2, seq, HEAD_DIM), lambda b, hp, i: (hp, b, 0)),
            pl.BlockSpec((2, VT_ROWS, seq), lambda b, hp, i: (hp, 0, b)),
            pl.BlockSpec((2, ctx_len, HEAD_DIM), lambda b, hp, i: (hp, cb + b, 0)),
            pl.BlockSpec((2, VT_ROWS, ctx_len), lambda b, hp, i: (hp, 0, cb + b)),
            pl.BlockSpec((3, 2, kk, tq), lambda b, hp, i: (0, hp, 0, 0)),
        ],
        out_specs=pl.BlockSpec((tb, 2 * HEAD_DIM), lambda b, hp, i: (b * nq + i, hp)),
        out_shape=jax.ShapeDtypeStruct((r, C_HEADS * HEAD_DIM), BF16),
        scratch_shapes=[pltpu.VMEM((2, kk + ctx_len, tq), F32)],
        compiler_params=_cparams(3),
        name="attn_nbr",
    )(q, k, vt, k, vt, bias)


def _ctx_kernel(sink_ref, qw_ref, qg_ref, qn_ref, kw_ref, kg_ref, kn_ref, vw_ref, vg_ref, vn_ref,
                yw_o, yg_o, yn_o):
    def group(q_ref, k_ref, vt_ref, o_ref, n_heads, n_kv, sink=False):
        outs = []
        for h in range(n_heads):
            kv = h // (n_heads // n_kv)
            s = _dot_nt(k_ref[kv], q_ref[h])
            outs.append(_softmax_pv_t([(s, vt_ref[kv])],
                                      extra_logit=sink_ref[0, h] * LOG2_E if sink else None))
        o_ref[...] = jnp.concatenate(outs, axis=0).T.astype(o_ref.dtype)

    group(qw_ref, kw_ref, vw_ref, yw_o, A_HEADS, A_KV, sink=True)
    group(qg_ref, kg_ref, vg_ref, yg_o, B_HEADS, B_KV)
    group(qn_ref, kn_ref, vn_ref, yn_o, C_HEADS, C_HEADS)


def _attn_ctx(sink, qs, ks, vs, *, n_batch, seq, ctx_len):
    cb = n_batch * seq // ctx_len

    def hspec(a):
        if a.shape[1] == VT_ROWS:
            return pl.BlockSpec((a.shape[0], VT_ROWS, ctx_len), lambda b: (0, 0, cb + b))
        return pl.BlockSpec((a.shape[0], ctx_len, HEAD_DIM), lambda b: (0, cb + b, 0))

    widths = [q.shape[0] * HEAD_DIM for q in qs]
    return pl.pallas_call(
        _ctx_kernel,
        grid=(n_batch,),
        in_specs=[pl.BlockSpec(memory_space=pltpu.SMEM)] + [hspec(a) for a in (*qs, *ks, *vs)],
        out_specs=[pl.BlockSpec((ctx_len, w), lambda b: (b, 0)) for w in widths],
        out_shape=[jax.ShapeDtypeStruct((n_batch * ctx_len, w), BF16) for w in widths],
        compiler_params=_cparams(1),
        name="attn_ctx",
    )(sink.reshape(1, A_HEADS), *qs, *ks, *vs)


def _rope_tables(seq):
    t = jnp.arange(seq)
    row = (t // GRID_W).astype(F32)
    col = (t % GRID_W).astype(F32)
    n_freq = HEAD_DIM // 4
    inv = ROPE_THETA ** (-jnp.arange(n_freq, dtype=F32) / n_freq)
    ang = jnp.concatenate([row[:, None] * inv, col[:, None] * inv], axis=-1)
    cos, sin = jnp.cos(ang), jnp.sin(ang)
    cosf = jnp.concatenate([jnp.tile(cos, (1, 4)), jnp.ones((TM, LANES), F32)], axis=0)
    sinf = jnp.concatenate([jnp.tile(jnp.concatenate([-sin, sin], axis=-1), (1, 2)),
                            jnp.zeros((TM, LANES), F32)], axis=0)
    return cosf, sinf


def kernel(x, c, ctx, c_ctx, w_ada, b_ada, norm_ffn1, w_ffn1_gate, w_ffn1_up, w_ffn1_down,
           norm_mix, w_in, q_norm_glob, k_norm_glob, sink_win, rpb_nbr, w_out,
           norm_ffn2, w_ffn2_gate, w_ffn2_up, w_ffn2_down, norm_final):
    n_batch, seq, d = x.shape
    ctx_len = ctx.shape[1]
    depth = w_ada.shape[0]
    r_lat, r_ctx = n_batch * seq, n_batch * ctx_len
    assert seq % TM == 0 and r_ctx % TM == 0 and n_batch + 1 <= MOD_ROWS
    assert seq % (min(WIN_TILES, seq // TQ_WIN) * TQ_WIN) == 0
    assert seq % TQ_GLOB == 0 and seq % (2 * TK_GLOB) == 0
    assert seq % (NBR_ROWS * GRID_W) == 0 and r_lat % ctx_len == 0 and seq >= TQ_WIN + 2 * A_WINDOW
    n_lat, n_all, tpb = r_lat // TM, (r_lat + r_ctx) // TM, seq // TM
    tiles = dict(n_lat=n_lat, tpb=tpb, n_batch=n_batch)
    dims = dict(n_batch=n_batch, seq=seq, ctx_len=ctx_len)

    h_srcs = (x.reshape(r_lat, d), ctx.reshape(r_ctx, d))
    cc = jnp.zeros((MOD_ROWS, d), F32).at[:n_batch].set(c).at[n_batch].set(c_ctx)
    mod = _ada(cc, w_ada, b_ada).reshape(depth, MOD_ROWS, N_MOD, d)
    cosf, sinf = _rope_tables(seq)
    seg = jnp.asarray(np.kron(np.eye(2), np.full((HEAD_DIM, HEAD_DIM), 1.0 / HEAD_DIM)), BF16)

    for l in range(depth):
        last = l == depth - 1
        h = _ffn(h_srcs, mod[l], norm_ffn1[l], w_ffn1_gate[l].astype(BF16), w_ffn1_up[l].astype(BF16),
                 w_ffn1_down[l].astype(BF16), norm_final, k0=0, n_tiles=n_all, final=False, **tiles)
        qw, qg, qn, kw, kg, kn, vw, vg, vn = _qkv(
            h, mod[l], norm_mix[l], w_in[l].astype(BF16), q_norm_glob[l], k_norm_glob[l],
            cosf, sinf, seg, n_tiles=n_all, **tiles)
        ys = [(_attn_window(qw, kw, vw, sink_win[l], **dims),),
              (_attn_global(qg, kg, vg, **dims),),
              (_attn_nbr(qn, kn, vn, _nbr_bias(rpb_nbr[l], seq // GRID_W), **dims),)]
        if not last:
            ycs = _attn_ctx(sink_win[l], (qw, qg, qn), (kw, kg, kn), (vw, vg, vn), **dims)
            ys = [(y, yc) for (y,), yc in zip(ys, ycs)]
        h = _ffn((h,), mod[l], norm_ffn2[l], w_ffn2_gate[l].astype(BF16), w_ffn2_up[l].astype(BF16),
                 w_ffn2_down[l].astype(BF16), norm_final, k0=6, n_tiles=n_lat if last else n_all,
                 final=last, y_srcs=ys, w_out=w_out[l].astype(BF16), **tiles)
        h_srcs = (h,)
    return h.reshape(n_batch, seq, d)
```

```python
import functools

import numpy as np
import jax
import jax.numpy as jnp
from jax import lax
from jax.experimental import pallas as pl
from jax.experimental.pallas import tpu as pltpu

D_MODEL = 1024
HEAD_DIM = 64
GRID_W = 64
A_HEADS, A_KV, A_WINDOW = 6, 2, 128
B_HEADS, B_KV = 4, 2
C_HEADS = 6
NA_KH, NA_KW = 8, 16
ROPE_THETA = 10000.0
EPS = 1e-6
N_MOD = 9
NEG = -1e30
MIX_WIDTH = D_MODEL
QK_SCALE = HEAD_DIM ** -0.5
LOG2_E = 1.4426950408889634
VT_ROWS = HEAD_DIM + 16
GLOB_SAFE_BOUND = 40.0

LANES = 128
MOD_ROWS = 8
TM = 512
TQ_WIN = 256
WIN_TILES = 8
TQ_GLOB = 1024
TK_GLOB = 512
NBR_ROWS = 4
NBR_TILES = 8
NBR_KROWS = NBR_ROWS + NA_KH
VMEM_LIMIT = 56 * 1024 * 1024

F32 = jnp.float32
BF16 = jnp.bfloat16


def _cparams(n_axes):
    return pltpu.CompilerParams(dimension_semantics=("arbitrary",) * n_axes,
                                vmem_limit_bytes=VMEM_LIMIT)


def _dot(a, b):
    return jnp.dot(a, b, preferred_element_type=F32)


def _dot_nt(a, b):
    return lax.dot_general(a, b, (((1,), (1,)), ((), ())), preferred_element_type=F32)


def _rms(x, g):
    return x * lax.rsqrt(jnp.mean(x * x, axis=-1, keepdims=True) + EPS) * g


def _ada_kernel(cc_ref, w_ref, b_ref, o_ref):
    a = cc_ref[...]
    a = a * jax.nn.sigmoid(a)
    o_ref[0] = jnp.dot(a, w_ref[0], preferred_element_type=F32,
                       precision=lax.Precision.HIGHEST) + b_ref[0]


def _ada(cc, w_ada, b_ada):
    depth, d, n = w_ada.shape
    tn = 1024
    return pl.pallas_call(
        _ada_kernel,
        grid=(depth, n // tn),
        in_specs=[
            pl.BlockSpec((MOD_ROWS, d), lambda l, j: (0, 0)),
            pl.BlockSpec((1, d, tn), lambda l, j: (l, 0, j)),
            pl.BlockSpec((1, 1, tn), lambda l, j: (l, 0, j)),
        ],
        out_specs=pl.BlockSpec((1, MOD_ROWS, tn), lambda l, j: (l, 0, j)),
        out_shape=jax.ShapeDtypeStruct((depth, MOD_ROWS, n), F32),
        compiler_params=_cparams(2),
        name="ada",
    )(cc, w_ada, b_ada.reshape(depth, 1, n))


def _pick_tile(refs, n_lat):
    if len(refs) == 1:
        return refs[0][...]
    return jnp.where(pl.program_id(0) < n_lat, refs[0][...], refs[1][...])


def _ffn_kernel(*refs, k0, final, n_h, n_y, n_lat):
    refs = list(refs)
    h_refs = [refs.pop(0) for _ in range(n_h)]
    y_refs = [[refs.pop(0) for _ in range(n_y)] for _ in range(3 if n_y else 0)]
    mod_ref, g_ref, wg_ref, wu_ref, wd_ref, gf_ref = (refs.pop(0) for _ in range(6))
    wo_ref = refs.pop(0) if n_y else None
    (o_ref,) = refs
    h = _pick_tile(h_refs, n_lat)
    mod = mod_ref[0]
    if n_y:
        row = 0
        y = None
        for yr in y_refs:
            width = yr[0].shape[1]
            part = _dot(_pick_tile(yr, n_lat), wo_ref[row:row + width])
            y = part if y is None else y + part
            row += width
        h = h + mod[5:6] * y
    u = _rms(h, g_ref[...]) * (1.0 + mod[k0 + 1:k0 + 2]) + mod[k0:k0 + 1]
    ub = u.astype(BF16)
    gate = _dot(ub, wg_ref[...])
    up = _dot(ub, wu_ref[...])
    a = (gate * jax.nn.sigmoid(gate) * up).astype(BF16)
    y = _dot(a, wd_ref[...])
    out = h + 0.5 * mod[k0 + 2:k0 + 3] * y
    if final:
        out = _rms(out, gf_ref[...])
    o_ref[...] = out


def _row_maps(layer, n_lat, tpb, n_batch):
    def mod_map(i):
        return (layer, jnp.where(i < n_lat, i // tpb, n_batch), 0, 0)

    def rope_map(i):
        return (jnp.where(i < n_lat, i % tpb, tpb), 0)

    return mod_map, rope_map


def _resident(shape, layer=None):
    nd = len(shape)
    if layer is None:
        return pl.BlockSpec(shape, lambda i: (0,) * nd, pipeline_mode=pl.Buffered(1))
    return pl.BlockSpec((None, *shape[1:]), lambda i: (layer,) + (0,) * (nd - 1), pipeline_mode=pl.Buffered(1))


def _tile_specs(arrays, n_lat):
    if len(arrays) == 1:
        return [pl.BlockSpec((TM, arrays[0].shape[1]), lambda i: (i, 0))]
    return [pl.BlockSpec((TM, arrays[0].shape[1]), lambda i: (jnp.minimum(i, n_lat - 1), 0)),
            pl.BlockSpec((TM, arrays[1].shape[1]), lambda i: (jnp.maximum(i - n_lat, 0), 0))]


def _ffn(h_srcs, mod, gain, wg, wu, wd, gain_final, *, layer, k0, n_tiles, n_lat, tpb, n_batch, final,
         y_srcs=(), w_out=None):
    depth, d, f = wg.shape
    mod_map, _ = _row_maps(layer, n_lat, tpb, n_batch)
    n_y = len(y_srcs[0]) if y_srcs else 0
    streams = [h_srcs, *y_srcs]
    return pl.pallas_call(
        functools.partial(_ffn_kernel, k0=k0, final=final, n_h=len(h_srcs), n_y=n_y, n_lat=n_lat),
        grid=(n_tiles,),
        in_specs=[spec for s in streams for spec in _tile_specs(s, n_lat)] + [
            pl.BlockSpec((None, 1, N_MOD, d), mod_map),
            _resident((depth, 1, d), layer),
            _resident(wg.shape, layer),
            _resident(wu.shape, layer),
            _resident(wd.shape, layer),
            _resident((1, d)),
        ] + ([_resident(w_out.shape, layer)] if n_y else []),
        out_specs=pl.BlockSpec((TM, d), lambda i: (i, 0)),
        out_shape=jax.ShapeDtypeStruct((n_tiles * TM, d), F32),
        compiler_params=_cparams(1),
        name="ffn",
    )(*[a for s in streams for a in s], mod, gain.reshape(depth, 1, d), wg, wu, wd, gain_final.reshape(1, d),
      *([w_out] if n_y else []))


def _swap_halves(x):
    lane = lax.broadcasted_iota(jnp.int32, x.shape, 1)
    return jnp.where((lane % HEAD_DIM) < HEAD_DIM // 2,
                     pltpu.roll(x, LANES - HEAD_DIM // 2, 1),
                     pltpu.roll(x, HEAD_DIM // 2, 1))


def _head_mean_sq(x, seg):
    y = x * x
    hi = y.astype(BF16)
    lo = (y - hi.astype(F32)).astype(BF16)
    return _dot(hi, seg) + _dot(lo, seg)


def _qkv_kernel(h_ref, mod_ref, g_ref, w_ref, qg_ref, kg_ref, cos_ref, sin_ref, seg_ref,
                qw_o, qgl_o, qn_o, kw_o, kgl_o, kn_o, vw_o, vgl_o, vn_o):
    h = h_ref[...]
    mod = mod_ref[0]
    u = _rms(h, g_ref[...]) * (1.0 + mod[4:5]) + mod[3:4]
    p = _dot(u.astype(BF16), w_ref[...])
    cosf = cos_ref[...]
    sinf = sin_ref[...]
    seg = seg_ref[...]

    def emit(col, out_ref, n_heads, gain_ref, rope, scale, transpose=False):
        for c in range(n_heads // 2):
            xc = p[:, col + c * LANES: col + (c + 1) * LANES]
            if gain_ref is not None:
                xc = xc * lax.rsqrt(_head_mean_sq(xc, seg) + EPS) * gain_ref[...]
            if rope:
                xc = xc * cosf + _swap_halves(xc) * sinf
            if scale is not None:
                xc = xc * scale
            if transpose:
                xt = xc.T
                ones = jnp.ones((VT_ROWS - HEAD_DIM, xt.shape[1]), BF16)
                for j in range(2):
                    out_ref[2 * c + j, :HEAD_DIM, :] = xt[j * HEAD_DIM:(j + 1) * HEAD_DIM].astype(BF16)
                    out_ref[2 * c + j, HEAD_DIM:, :] = ones
            else:
                out_ref[2 * c] = xc[:, :HEAD_DIM].astype(BF16)
                out_ref[2 * c + 1] = xc[:, HEAD_DIM:].astype(BF16)

    col = 0
    q_scale = QK_SCALE * LOG2_E
    emit(col, qw_o, A_HEADS, None, True, q_scale)
    col += A_HEADS * HEAD_DIM
    emit(col, qgl_o, B_HEADS, qg_ref, True, q_scale)
    col += B_HEADS * HEAD_DIM
    emit(col, qn_o, C_HEADS, None, False, q_scale)
    col += C_HEADS * HEAD_DIM
    emit(col, kw_o, A_KV, None, True, None)
    col += A_KV * HEAD_DIM
    emit(col, vw_o, A_KV, None, False, None, transpose=True)
    col += A_KV * HEAD_DIM
    emit(col, kgl_o, B_KV, kg_ref, True, None)
    col += B_KV * HEAD_DIM
    emit(col, vgl_o, B_KV, None, False, None, transpose=True)
    col += B_KV * HEAD_DIM
    emit(col, kn_o, C_HEADS, None, False, None)
    col += C_HEADS * HEAD_DIM
    emit(col, vn_o, C_HEADS, None, False, None, transpose=True)


def _qkv(h, mod, gain, w_in, q_gain, k_gain, cosf, sinf, seg, *, layer, n_tiles, n_lat, tpb, n_batch):
    r, d = h.shape
    depth = w_in.shape[0]
    mod_map, rope_map = _row_maps(layer, n_lat, tpb, n_batch)
    heads = (A_HEADS, B_HEADS, C_HEADS, A_KV, B_KV, C_HEADS, A_KV, B_KV, C_HEADS)
    transposed = tuple(j >= 6 for j in range(9))
    return pl.pallas_call(
        _qkv_kernel,
        grid=(n_tiles,),
        in_specs=[
            pl.BlockSpec((TM, d), lambda i: (i, 0)),
            pl.BlockSpec((None, 1, N_MOD, d), mod_map),
            _resident((depth, 1, d), layer),
            _resident(w_in.shape, layer),
            _resident((1, LANES)),
            _resident((1, LANES)),
            pl.BlockSpec((TM, LANES), rope_map),
            pl.BlockSpec((TM, LANES), rope_map),
            _resident((LANES, LANES)),
        ],
        out_specs=[pl.BlockSpec((n, VT_ROWS, TM), lambda i: (0, 0, i)) if t else
                   pl.BlockSpec((n, TM, HEAD_DIM), lambda i: (0, i, 0)) for n, t in zip(heads, transposed)],
        out_shape=[jax.ShapeDtypeStruct((n, VT_ROWS, r) if t else (n, r, HEAD_DIM), BF16)
                   for n, t in zip(heads, transposed)],
        compiler_params=_cparams(1),
        name="qkv",
    )(h, mod, gain.reshape(depth, 1, d), w_in, jnp.tile(q_gain, 2).reshape(1, LANES),
      jnp.tile(k_gain, 2).reshape(1, LANES), cosf, sinf, seg)


def _softmax_pv_t(parts, extra_logit=None):
    m = functools.reduce(jnp.maximum, [jnp.max(s, axis=0, keepdims=True) for s, _ in parts])
    if extra_logit is not None:
        m = jnp.maximum(m, extra_logit)
    acc = None
    for s, vt in parts:
        pv = _dot(vt, jnp.exp2(s - m).astype(BF16))
        acc = pv if acc is None else acc + pv
    den = acc[HEAD_DIM:HEAD_DIM + 1]
    if extra_logit is not None:
        den = den + jnp.exp2(extra_logit - m)
    return acc[:HEAD_DIM] / den


def _win_kernel(sink_ref, q_ref, k_ref, vt_ref, kc_ref, vtc_ref, o_ref, s_scr, mask_scr, *, seq):
    tq = TQ_WIN
    n_t = q_ref.shape[1] // tq
    grp = A_HEADS // A_KV
    kb = tq + 2 * A_WINDOW
    tile0 = pl.program_id(1) * n_t
    last_tile = seq // tq - 1
    sinks = [jnp.concatenate([jnp.full((1, tq), sink_ref[0, kv * grp + g] * LOG2_E, F32)
                              for g in range(grp)], axis=1) for kv in range(A_KV)]

    @pl.when((pl.program_id(0) == 0) & (pl.program_id(1) == 0))
    def _():
        rel = (lax.broadcasted_iota(jnp.int32, (kb, tq), 0) - lax.broadcasted_iota(jnp.int32, (kb, tq), 1))
        for cls in range(3):
            ok = jnp.abs(rel - cls * A_WINDOW) <= A_WINDOW
            mask_scr[cls] = jnp.concatenate([jnp.where(ok, 0.0, NEG)] * grp, axis=1)

    def band(t):
        g = tile0 + t
        q0 = g * tq
        cls = jnp.where(g == 0, 0, jnp.where(g == last_tile, 2, 1))
        return cls, pl.multiple_of(jnp.clip(q0 - A_WINDOW, 0, seq - kb), A_WINDOW)

    def scores(slot, t, kv):
        cls, ks = band(t)
        q = q_ref[kv * grp:(kv + 1) * grp, pl.ds(pl.multiple_of(t * tq, tq), tq), :]
        q = q.reshape(grp * tq, HEAD_DIM)
        s_scr[slot, :kb, :] = _dot_nt(k_ref[kv, pl.ds(ks, kb), :], q) + mask_scr[cls]
        s_scr[slot, kb:, :] = _dot_nt(kc_ref[kv], q)

    def absorb(slot, t, kv):
        _, ks = band(t)
        s = s_scr[slot]
        m = jnp.maximum(jnp.max(s, axis=0, keepdims=True), sinks[kv])
        e = jnp.exp2(s - m).astype(BF16)
        acc = _dot(vt_ref[kv, :, pl.ds(ks, kb)], e[:kb]) + _dot(vtc_ref[kv], e[kb:])
        o = acc[:HEAD_DIM] / (acc[HEAD_DIM:HEAD_DIM + 1] + jnp.exp2(sinks[kv] - m))
        return [o[:, g * tq:(g + 1) * tq] for g in range(grp)]

    scores(0, 0, 0)

    def body(t, carry):
        scores(1, t, 1)
        outs = absorb(0, t, 0)
        scores(0, jnp.minimum(t + 1, n_t - 1), 0)
        outs += absorb(1, t, 1)
        o_ref[pl.ds(pl.multiple_of(t * tq, tq), tq), :] = jnp.concatenate(outs, axis=0).T.astype(o_ref.dtype)
        return carry

    lax.fori_loop(0, n_t, body, 0)


def _attn_window(q, k, vt, sink, *, n_batch, seq, ctx_len):
    r = n_batch * seq
    n_t = min(WIN_TILES, seq // TQ_WIN)
    tb = n_t * TQ_WIN
    nq = seq // tb
    cb = n_batch * seq // ctx_len
    return pl.pallas_call(
        functools.partial(_win_kernel, seq=seq),
        grid=(n_batch, nq),
        in_specs=[
            pl.BlockSpec(memory_space=pltpu.SMEM),
            pl.BlockSpec((A_HEADS, tb, HEAD_DIM), lambda b, i: (0, b * nq + i, 0)),
            pl.BlockSpec((A_KV, seq, HEAD_DIM), lambda b, i: (0, b, 0)),
            pl.BlockSpec((A_KV, VT_ROWS, seq), lambda b, i: (0, 0, b)),
            pl.BlockSpec((A_KV, ctx_len, HEAD_DIM), lambda b, i: (0, cb + b, 0)),
            pl.BlockSpec((A_KV, VT_ROWS, ctx_len), lambda b, i: (0, 0, cb + b)),
        ],
        out_specs=pl.BlockSpec((tb, A_HEADS * HEAD_DIM), lambda b, i: (b * nq + i, 0)),
        out_shape=jax.ShapeDtypeStruct((r, A_HEADS * HEAD_DIM), BF16),
        scratch_shapes=[pltpu.VMEM((2, TQ_WIN + 2 * A_WINDOW + ctx_len, (A_HEADS // A_KV) * TQ_WIN), F32),
                        pltpu.VMEM((3, TQ_WIN + 2 * A_WINDOW, (A_HEADS // A_KV) * TQ_WIN), F32)],
        compiler_params=_cparams(2),
        name="attn_window",
    )(sink.reshape(1, A_HEADS), q, k, vt, k, vt)


def _glob_kernel(bound_ref, q_ref, k_ref, vt_ref, kc_ref, vtc_ref, o_ref, s_scr, *, seq):
    tq = q_ref.shape[1]
    grp = B_HEADS // B_KV
    n_chunks = seq // TK_GLOB
    bound = bound_ref[0, 0]

    def queries(kv):
        return q_ref[kv * grp:(kv + 1) * grp].reshape(grp * tq, HEAD_DIM)

    def scores(c, kv):
        off = pl.multiple_of(c * TK_GLOB, TK_GLOB)
        return _dot_nt(k_ref[kv, pl.ds(off, TK_GLOB), :], queries(kv))

    def vt_chunk(c, kv):
        return vt_ref[kv, :, pl.ds(pl.multiple_of(c * TK_GLOB, TK_GLOB), TK_GLOB)]

    def finish(accs):
        outs = []
        for acc in accs:
            o = acc[:HEAD_DIM] / acc[HEAD_DIM:HEAD_DIM + 1]
            outs.extend(o[:, g * tq:(g + 1) * tq] for g in range(grp))
        o_ref[...] = jnp.concatenate(outs, axis=0).T.astype(o_ref.dtype)

    @pl.when(bound <= GLOB_SAFE_BOUND)
    def _():
        def weights(s):
            return jnp.exp2(s - bound).astype(BF16)

        acc0 = tuple(_dot(vtc_ref[kv], weights(_dot_nt(kc_ref[kv], queries(kv)))) for kv in range(B_KV))

        def body(c, accs):
            return tuple(acc + _dot(vt_chunk(c, kv), weights(scores(c, kv))) for kv, acc in enumerate(accs))

        finish(lax.fori_loop(0, n_chunks, body, acc0, unroll=4))

    @pl.when(bound > GLOB_SAFE_BOUND)
    def _():
        def absorb(s, vt, m, acc):
            m_new = jnp.maximum(m, jnp.max(s, axis=0, keepdims=True))
            e = jnp.exp2(s - m_new).astype(BF16)
            return m_new, jnp.exp2(m - m_new) * acc + _dot(vt, e)

        accs = []
        for kv in range(B_KV):
            s = _dot_nt(kc_ref[kv], queries(kv))
            m0 = jnp.max(s, axis=0, keepdims=True)
            a0 = _dot(vtc_ref[kv], jnp.exp2(s - m0).astype(BF16))
            s_scr[0] = scores(0, kv)

            def body(j, carry, kv=kv):
                m, acc = carry
                s_scr[1] = scores(2 * j + 1, kv)
                m, acc = absorb(s_scr[0], vt_chunk(2 * j, kv), m, acc)
                s_scr[0] = scores(jnp.minimum(2 * j + 2, n_chunks - 1), kv)
                return absorb(s_scr[1], vt_chunk(2 * j + 1, kv), m, acc)

            accs.append(lax.fori_loop(0, n_chunks // 2, body, (m0, a0))[1])
        finish(accs)


def _logit_bound(q_gain, k_gain):
    bound = (HEAD_DIM * QK_SCALE * LOG2_E * 1.02) * jnp.max(jnp.abs(q_gain)) * jnp.max(jnp.abs(k_gain))
    return bound.astype(F32).reshape(1, 1)


def _attn_global(q, k, vt, bound, *, n_batch, seq, ctx_len):
    r = n_batch * seq
    tq = TQ_GLOB
    nq = seq // tq
    cb = n_batch * seq // ctx_len
    return pl.pallas_call(
        functools.partial(_glob_kernel, seq=seq),
        grid=(n_batch, nq),
        in_specs=[
            pl.BlockSpec(memory_space=pltpu.SMEM),
            pl.BlockSpec((B_HEADS, tq, HEAD_DIM), lambda b, i: (0, b * nq + i, 0)),
            pl.BlockSpec((B_KV, seq, HEAD_DIM), lambda b, i: (0, b, 0)),
            pl.BlockSpec((B_KV, VT_ROWS, seq), lambda b, i: (0, 0, b)),
            pl.BlockSpec((B_KV, ctx_len, HEAD_DIM), lambda b, i: (0, cb + b, 0)),
            pl.BlockSpec((B_KV, VT_ROWS, ctx_len), lambda b, i: (0, 0, cb + b)),
        ],
        out_specs=pl.BlockSpec((tq, B_HEADS * HEAD_DIM), lambda b, i: (b * nq + i, 0)),
        out_shape=jax.ShapeDtypeStruct((r, B_HEADS * HEAD_DIM), BF16),
        scratch_shapes=[pltpu.VMEM((2, TK_GLOB, (B_HEADS // B_KV) * tq), F32)],
        compiler_params=_cparams(2),
        name="attn_global",
    )(bound, q, k, vt, k, vt)


def _nbr_kernel(q_ref, k_ref, vt_ref, kc_ref, vtc_ref, bias_ref, o_ref, s_scr, *, rows):
    tq = NBR_ROWS * GRID_W
    kk = NBR_KROWS * GRID_W
    n_t = q_ref.shape[1] // tq
    tile0 = pl.program_id(2) * n_t
    last_tile = rows // NBR_ROWS - 1

    def band(t):
        g = tile0 + t
        cls = jnp.where(g == 0, 0, jnp.where(g == last_tile, 2, 1))
        kr0 = jnp.clip(g * NBR_ROWS - NA_KH // 2, 0, rows - NBR_KROWS)
        return cls, pl.multiple_of(kr0 * GRID_W, LANES)

    def scores(slot, t, h):
        cls, ks = band(t)
        q = q_ref[h, pl.ds(pl.multiple_of(t * tq, tq), tq), :]
        s_scr[slot, :kk, :] = _dot_nt(k_ref[h, pl.ds(ks, kk), :], q) + bias_ref[cls, h]
        s_scr[slot, kk:, :] = _dot_nt(kc_ref[h], q)

    def absorb(slot, t, h):
        _, ks = band(t)
        s = s_scr[slot]
        e = jnp.exp2(s - jnp.max(s, axis=0, keepdims=True)).astype(BF16)
        acc = _dot(vt_ref[h, :, pl.ds(ks, kk)], e[:kk]) + _dot(vtc_ref[h], e[kk:])
        return acc[:HEAD_DIM] / acc[HEAD_DIM:HEAD_DIM + 1]

    scores(0, 0, 0)

    def body(t, carry):
        scores(1, t, 1)
        o0 = absorb(0, t, 0)
        scores(0, jnp.minimum(t + 1, n_t - 1), 0)
        o1 = absorb(1, t, 1)
        o_ref[pl.ds(pl.multiple_of(t * tq, tq), tq), :] = jnp.concatenate([o0, o1], axis=0).T.astype(o_ref.dtype)
        return carry

    lax.fori_loop(0, n_t, body, 0)


def _nbr_bias_kernel(rpb_ref, o_ref, *, rows):
    h = pl.program_id(0)
    n_dc = 2 * NA_KW - 1
    kcol = lax.broadcasted_iota(jnp.int32, (GRID_W, GRID_W), 0)
    col = lax.broadcasted_iota(jnp.int32, (GRID_W, GRID_W), 1)
    dc = kcol - col + NA_KW - 1
    cs = jnp.clip(col - NA_KW // 2, 0, GRID_W - NA_KW)
    col_ok = (kcol >= cs) & (kcol < cs + NA_KW)
    neg = jnp.full((GRID_W, GRID_W), NEG, F32)
    by_dr = []
    for dr in range(2 * NA_KH - 1):
        blk = neg
        for j in range(n_dc):
            blk = jnp.where(dc == j, rpb_ref[h, dr * n_dc + j] * LOG2_E, blk)
        by_dr.append(jnp.where(col_ok, blk, NEG))
    for cls, r0 in enumerate((0, NBR_ROWS, rows - NBR_ROWS)):
        kr0 = min(max(r0 - NA_KH // 2, 0), rows - NBR_KROWS)
        for kri in range(NBR_KROWS):
            krow = kr0 + kri
            blocks = []
            for ri in range(NBR_ROWS):
                rs = min(max(r0 + ri - NA_KH // 2, 0), rows - NA_KH)
                blocks.append(by_dr[krow - r0 - ri + NA_KH - 1] if rs <= krow < rs + NA_KH else neg)
            o_ref[cls, 0, kri * GRID_W:(kri + 1) * GRID_W, :] = jnp.concatenate(blocks, axis=-1)


def _nbr_bias(rpb, rows):
    n_heads = rpb.shape[0]
    tq, kk = NBR_ROWS * GRID_W, NBR_KROWS * GRID_W
    return pl.pallas_call(
        functools.partial(_nbr_bias_kernel, rows=rows),
        grid=(n_heads,),
        in_specs=[pl.BlockSpec(memory_space=pltpu.SMEM)],
        out_specs=pl.BlockSpec((3, 1, kk, tq), lambda h: (0, h, 0, 0)),
        out_shape=jax.ShapeDtypeStruct((3, n_heads, kk, tq), F32),
        compiler_params=_cparams(1),
        name="nbr_bias",
    )(rpb.astype(F32).reshape(n_heads, -1))


def _attn_nbr(q, k, vt, bias, *, n_batch, seq, ctx_len):
    r = n_batch * seq
    rows = seq // GRID_W
    tq = NBR_ROWS * GRID_W
    kk = NBR_KROWS * GRID_W
    tb = min(NBR_TILES, seq // tq) * tq
    nq = seq // tb
    cb = n_batch * seq // ctx_len
    return pl.pallas_call(
        functools.partial(_nbr_kernel, rows=rows),
        grid=(n_batch, C_HEADS // 2, nq),
        in_specs=[
            pl.BlockSpec((2, tb, HEAD_DIM), lambda b, hp, i: (hp, b * nq + i, 0)),
            pl.BlockSpec((2, seq, HEAD_DIM), lambda b, hp, i: (hp, b, 0)),
            pl.BlockSpec((2, VT_ROWS, seq), lambda b, hp, i: (hp, 0, b)),
            pl.BlockSpec((2, ctx_len, HEAD_DIM), lambda b, hp, i: (hp, cb + b, 0)),
            pl.BlockSpec((2, VT_ROWS, ctx_len), lambda b, hp, i: (hp, 0, cb + b)),
            pl.BlockSpec((3, 2, kk, tq), lambda b, hp, i: (0, hp, 0, 0)),
        ],
        out_specs=pl.BlockSpec((tb, 2 * HEAD_DIM), lambda b, hp, i: (b * nq + i, hp)),
        out_shape=jax.ShapeDtypeStruct((r, C_HEADS * HEAD_DIM), BF16),
        scratch_shapes=[pltpu.VMEM((2, kk + ctx_len, tq), F32)],
        compiler_params=_cparams(3),
        name="attn_nbr",
    )(q, k, vt, k, vt, bias)


def _ctx_kernel(sink_ref, qw_ref, qg_ref, qn_ref, kw_ref, kg_ref, kn_ref, vw_ref, vg_ref, vn_ref,
                yw_o, yg_o, yn_o):
    def group(q_ref, k_ref, vt_ref, o_ref, n_heads, n_kv, sink=False):
        outs = []
        for h in range(n_heads):
            kv = h // (n_heads // n_kv)
            s = _dot_nt(k_ref[kv], q_ref[h])
            outs.append(_softmax_pv_t([(s, vt_ref[kv])],
                                      extra_logit=sink_ref[0, h] * LOG2_E if sink else None))
        o_ref[...] = jnp.concatenate(outs, axis=0).T.astype(o_ref.dtype)

    group(qw_ref, kw_ref, vw_ref, yw_o, A_HEADS, A_KV, sink=True)
    group(qg_ref, kg_ref, vg_ref, yg_o, B_HEADS, B_KV)
    group(qn_ref, kn_ref, vn_ref, yn_o, C_HEADS, C_HEADS)


def _attn_ctx(sink, qs, ks, vs, *, n_batch, seq, ctx_len):
    cb = n_batch * seq // ctx_len

    def hspec(a):
        if a.shape[1] == VT_ROWS:
            return pl.BlockSpec((a.shape[0], VT_ROWS, ctx_len), lambda b: (0, 0, cb + b))
        return pl.BlockSpec((a.shape[0], ctx_len, HEAD_DIM), lambda b: (0, cb + b, 0))

    widths = [q.shape[0] * HEAD_DIM for q in qs]
    return pl.pallas_call(
        _ctx_kernel,
        grid=(n_batch,),
        in_specs=[pl.BlockSpec(memory_space=pltpu.SMEM)] + [hspec(a) for a in (*qs, *ks, *vs)],
        out_specs=[pl.BlockSpec((ctx_len, w), lambda b: (b, 0)) for w in widths],
        out_shape=[jax.ShapeDtypeStruct((n_batch * ctx_len, w), BF16) for w in widths],
        compiler_params=_cparams(1),
        name="attn_ctx",
    )(sink.reshape(1, A_HEADS), *qs, *ks, *vs)


def _rope_tables(seq):
    t = jnp.arange(seq)
    row = (t // GRID_W).astype(F32)
    col = (t % GRID_W).astype(F32)
    n_freq = HEAD_DIM // 4
    inv = ROPE_THETA ** (-jnp.arange(n_freq, dtype=F32) / n_freq)
    ang = jnp.concatenate([row[:, None] * inv, col[:, None] * inv], axis=-1)
    cos, sin = jnp.cos(ang), jnp.sin(ang)
    cosf = jnp.concatenate([jnp.tile(cos, (1, 4)), jnp.ones((TM, LANES), F32)], axis=0)
    sinf = jnp.concatenate([jnp.tile(jnp.concatenate([-sin, sin], axis=-1), (1, 2)),
                            jnp.zeros((TM, LANES), F32)], axis=0)
    return cosf, sinf


def kernel(x, c, ctx, c_ctx, w_ada, b_ada, norm_ffn1, w_ffn1_gate, w_ffn1_up, w_ffn1_down,
           norm_mix, w_in, q_norm_glob, k_norm_glob, sink_win, rpb_nbr, w_out,
           norm_ffn2, w_ffn2_gate, w_ffn2_up, w_ffn2_down, norm_final):
    n_batch, seq, d = x.shape
    ctx_len = ctx.shape[1]
    depth = w_ada.shape[0]
    r_lat, r_ctx = n_batch * seq, n_batch * ctx_len
    assert seq % TM == 0 and r_ctx % TM == 0 and n_batch + 1 <= MOD_ROWS
    assert seq % (min(WIN_TILES, seq // TQ_WIN) * TQ_WIN) == 0
    assert seq % TQ_GLOB == 0 and seq % (2 * TK_GLOB) == 0
    assert seq % (NBR_ROWS * GRID_W) == 0 and r_lat % ctx_len == 0 and seq >= TQ_WIN + 2 * A_WINDOW
    n_lat, n_all, tpb = r_lat // TM, (r_lat + r_ctx) // TM, seq // TM
    tiles = dict(n_lat=n_lat, tpb=tpb, n_batch=n_batch)
    dims = dict(n_batch=n_batch, seq=seq, ctx_len=ctx_len)

    h_srcs = (x.reshape(r_lat, d), ctx.reshape(r_ctx, d))
    cc = jnp.zeros((MOD_ROWS, d), F32).at[:n_batch].set(c).at[n_batch].set(c_ctx)
    mod = _ada(cc, w_ada, b_ada).reshape(depth, MOD_ROWS, N_MOD, d)
    cosf, sinf = _rope_tables(seq)
    seg = jnp.asarray(np.kron(np.eye(2), np.full((HEAD_DIM, HEAD_DIM), 1.0 / HEAD_DIM)), BF16)

    ffn1 = [w.astype(BF16) for w in (w_ffn1_gate, w_ffn1_up, w_ffn1_down)]
    ffn2 = [w.astype(BF16) for w in (w_ffn2_gate, w_ffn2_up, w_ffn2_down)]
    w_in_b, w_out_b = w_in.astype(BF16), w_out.astype(BF16)

    for l in range(depth):
        last = l == depth - 1
        h = _ffn(h_srcs, mod, norm_ffn1, *ffn1, norm_final, layer=l, k0=0, n_tiles=n_all, final=False, **tiles)
        qw, qg, qn, kw, kg, kn, vw, vg, vn = _qkv(
            h, mod, norm_mix, w_in_b, q_norm_glob[l], k_norm_glob[l],
            cosf, sinf, seg, layer=l, n_tiles=n_all, **tiles)
        ys = [(_attn_window(qw, kw, vw, sink_win[l], **dims),),
              (_attn_global(qg, kg, vg, _logit_bound(q_norm_glob[l], k_norm_glob[l]), **dims),),
              (_attn_nbr(qn, kn, vn, _nbr_bias(rpb_nbr[l], seq // GRID_W), **dims),)]
        if not last:
            ycs = _attn_ctx(sink_win[l], (qw, qg, qn), (kw, kg, kn), (vw, vg, vn), **dims)
            ys = [(y, yc) for (y,), yc in zip(ys, ycs)]
        h = _ffn((h,), mod, norm_ffn2, *ffn2, norm_final, layer=l, k0=6, n_tiles=n_lat if last else n_all,
                 final=last, y_srcs=ys, w_out=w_out_b, **tiles)
        h_srcs = (h,)
    return h.reshape(n_batch, seq, d)
```

```python
import functools

import numpy as np
import jax
import jax.numpy as jnp
from jax import lax
from jax.experimental import pallas as pl
from jax.experimental.pallas import tpu as pltpu

D_MODEL = 1024
HEAD_DIM = 64
GRID_W = 64
A_HEADS, A_KV, A_WINDOW = 6, 2, 128
B_HEADS, B_KV = 4, 2
C_HEADS = 6
NA_KH, NA_KW = 8, 16
ROPE_THETA = 10000.0
EPS = 1e-6
N_MOD = 9
NEG = -1e30
MIX_WIDTH = D_MODEL
QK_SCALE = HEAD_DIM ** -0.5
LOG2_E = 1.4426950408889634
VT_ROWS = HEAD_DIM + 16
GLOB_SAFE_BOUND = 40.0

W_IN_HEADS = dict(qw=A_HEADS, qg=B_HEADS, qn=C_HEADS, kw=A_KV, vw=A_KV, kg=B_KV, vg=B_KV, kn=C_HEADS, vn=C_HEADS)

LANES = 128
MOD_ROWS = 8
TM = 512
TQ_WIN = 256
WIN_TILES = 8
WIN_UNROLL = 2
TQ_GLOB = 1024
TK_GLOB = 512
NBR_ROWS = 4
NBR_TILES = 8
NBR_UNROLL = 4
NBR_KROWS = NBR_ROWS + NA_KH
VMEM_LIMIT = 56 * 1024 * 1024

F32 = jnp.float32
BF16 = jnp.bfloat16


def _cparams(n_axes):
    return pltpu.CompilerParams(dimension_semantics=("arbitrary",) * n_axes,
                                vmem_limit_bytes=VMEM_LIMIT)


def _dot(a, b):
    return jnp.dot(a, b, preferred_element_type=F32)


def _dot_nt(a, b):
    return lax.dot_general(a, b, (((1,), (1,)), ((), ())), preferred_element_type=F32)


def _rms(x, g):
    return x * lax.rsqrt(jnp.mean(x * x, axis=-1, keepdims=True) + EPS) * g


def _ada_kernel(cc_ref, w_ref, b_ref, o_ref):
    a = cc_ref[...]
    a = a * jax.nn.sigmoid(a)
    o_ref[0] = jnp.dot(a, w_ref[0], preferred_element_type=F32,
                       precision=lax.Precision.HIGHEST) + b_ref[0]


def _ada(cc, w_ada, b_ada):
    depth, d, n = w_ada.shape
    tn = 1024
    return pl.pallas_call(
        _ada_kernel,
        grid=(depth, n // tn),
        in_specs=[
            pl.BlockSpec((MOD_ROWS, d), lambda l, j: (0, 0)),
            pl.BlockSpec((1, d, tn), lambda l, j: (l, 0, j)),
            pl.BlockSpec((1, 1, tn), lambda l, j: (l, 0, j)),
        ],
        out_specs=pl.BlockSpec((1, MOD_ROWS, tn), lambda l, j: (l, 0, j)),
        out_shape=jax.ShapeDtypeStruct((depth, MOD_ROWS, n), F32),
        compiler_params=_cparams(2),
        name="ada",
    )(cc, w_ada, b_ada.reshape(depth, 1, n))


def _pick_tile(refs, n_lat):
    if len(refs) == 1:
        return refs[0][...]
    return jnp.where(pl.program_id(0) < n_lat, refs[0][...], refs[1][...])


def _ffn_kernel(*refs, k0, final, n_h, n_y, n_lat):
    refs = list(refs)
    h_refs = [refs.pop(0) for _ in range(n_h)]
    y_refs = [[refs.pop(0) for _ in range(n_y)] for _ in range(3 if n_y else 0)]
    mod_ref, g_ref, wg_ref, wu_ref, wd_ref, gf_ref = (refs.pop(0) for _ in range(6))
    wo_ref = refs.pop(0) if n_y else None
    (o_ref,) = refs
    h = _pick_tile(h_refs, n_lat)
    mod = mod_ref[0]
    if n_y:
        row = 0
        y = None
        for yr in y_refs:
            width = yr[0].shape[1]
            part = _dot(_pick_tile(yr, n_lat), wo_ref[row:row + width])
            y = part if y is None else y + part
            row += width
        h = h + mod[5:6] * y
    u = _rms(h, g_ref[...]) * (1.0 + mod[k0 + 1:k0 + 2]) + mod[k0:k0 + 1]
    ub = u.astype(BF16)
    gate = _dot(ub, wg_ref[...])
    up = _dot(ub, wu_ref[...])
    a = (gate * jax.nn.sigmoid(gate) * up).astype(BF16)
    y = _dot(a, wd_ref[...])
    out = h + 0.5 * mod[k0 + 2:k0 + 3] * y
    if final:
        out = _rms(out, gf_ref[...])
    o_ref[...] = out


def _row_maps(layer, n_lat, tpb, n_batch):
    def mod_map(i):
        return (layer, jnp.where(i < n_lat, i // tpb, n_batch), 0, 0)

    def rope_map(i):
        return (jnp.where(i < n_lat, i % tpb, tpb), 0)

    return mod_map, rope_map


def _resident(shape, layer=None):
    nd = len(shape)
    if layer is None:
        return pl.BlockSpec(shape, lambda i: (0,) * nd, pipeline_mode=pl.Buffered(1))
    return pl.BlockSpec((None, *shape[1:]), lambda i: (layer,) + (0,) * (nd - 1), pipeline_mode=pl.Buffered(1))


def _tile_specs(arrays, n_lat):
    if len(arrays) == 1:
        return [pl.BlockSpec((TM, arrays[0].shape[1]), lambda i: (i, 0))]
    return [pl.BlockSpec((TM, arrays[0].shape[1]), lambda i: (jnp.minimum(i, n_lat - 1), 0)),
            pl.BlockSpec((TM, arrays[1].shape[1]), lambda i: (jnp.maximum(i - n_lat, 0), 0))]


def _ffn(h_srcs, mod, gain, wg, wu, wd, gain_final, *, layer, k0, n_tiles, n_lat, tpb, n_batch, final,
         y_srcs=(), w_out=None):
    depth, d, f = wg.shape
    mod_map, _ = _row_maps(layer, n_lat, tpb, n_batch)
    n_y = len(y_srcs[0]) if y_srcs else 0
    streams = [h_srcs, *y_srcs]
    return pl.pallas_call(
        functools.partial(_ffn_kernel, k0=k0, final=final, n_h=len(h_srcs), n_y=n_y, n_lat=n_lat),
        grid=(n_tiles,),
        in_specs=[spec for s in streams for spec in _tile_specs(s, n_lat)] + [
            pl.BlockSpec((None, 1, N_MOD, d), mod_map),
            _resident((depth, 1, d), layer),
            _resident(wg.shape, layer),
            _resident(wu.shape, layer),
            _resident(wd.shape, layer),
            _resident((1, d)),
        ] + ([_resident(w_out.shape, layer)] if n_y else []),
        out_specs=pl.BlockSpec((TM, d), lambda i: (i, 0)),
        out_shape=jax.ShapeDtypeStruct((n_tiles * TM, d), F32),
        compiler_params=_cparams(1),
        name="ffn",
    )(*[a for s in streams for a in s], mod, gain.reshape(depth, 1, d), wg, wu, wd, gain_final.reshape(1, d),
      *([w_out] if n_y else []))


def _swap_halves(x):
    lane = lax.broadcasted_iota(jnp.int32, x.shape, 1)
    return jnp.where((lane % HEAD_DIM) < HEAD_DIM // 2,
                     pltpu.roll(x, LANES - HEAD_DIM // 2, 1),
                     pltpu.roll(x, HEAD_DIM // 2, 1))


def _head_mean_sq(x, seg):
    y = x * x
    hi = y.astype(BF16)
    lo = (y - hi.astype(F32)).astype(BF16)
    return _dot(hi, seg) + _dot(lo, seg)


def _qkv_kernel(h_ref, mod_ref, g_ref, w_ref, qg_ref, kg_ref, cos_ref, sin_ref, seg_ref,
                qw_o, qgl_o, qn_o, kw_o, kgl_o, kn_o, vw_o, vgl_o, vn_o):
    h = h_ref[...]
    mod = mod_ref[0]
    u = _rms(h, g_ref[...]) * (1.0 + mod[4:5]) + mod[3:4]
    p = _dot(u.astype(BF16), w_ref[...])
    cosf = cos_ref[...]
    sinf = sin_ref[...]
    seg = seg_ref[...]

    def emit(col, out_ref, n_heads, gain_ref, rope, scale, transpose=False):
        for c in range(n_heads // 2):
            xc = p[:, col + c * LANES: col + (c + 1) * LANES]
            if gain_ref is not None:
                xc = xc * lax.rsqrt(_head_mean_sq(xc, seg) + EPS) * gain_ref[...]
            if rope:
                xc = xc * cosf + _swap_halves(xc) * sinf
            if scale is not None:
                xc = xc * scale
            if transpose:
                xt = xc.T
                ones = jnp.ones((VT_ROWS - HEAD_DIM, xt.shape[1]), BF16)
                for j in range(2):
                    out_ref[2 * c + j, :HEAD_DIM, :] = xt[j * HEAD_DIM:(j + 1) * HEAD_DIM].astype(BF16)
                    out_ref[2 * c + j, HEAD_DIM:, :] = ones
            else:
                out_ref[2 * c] = xc[:, :HEAD_DIM].astype(BF16)
                out_ref[2 * c + 1] = xc[:, HEAD_DIM:].astype(BF16)

    q_scale = QK_SCALE * LOG2_E
    plan = {
        "qw": (qw_o, None, True, q_scale, False), "qg": (qgl_o, qg_ref, True, q_scale, False),
        "qn": (qn_o, None, False, q_scale, False), "kw": (kw_o, None, True, None, False),
        "vw": (vw_o, None, False, None, True), "kg": (kgl_o, kg_ref, True, None, False),
        "vg": (vgl_o, None, False, None, True), "kn": (kn_o, None, False, None, False),
        "vn": (vn_o, None, False, None, True),
    }
    col = 0
    for name, n_heads in W_IN_HEADS.items():
        out_ref, gain_ref, rope, scale, transpose = plan[name]
        emit(col, out_ref, n_heads, gain_ref, rope, scale, transpose)
        col += n_heads * HEAD_DIM


def _qkv(h, mod, gain, w_in, q_gain, k_gain, cosf, sinf, seg, *, layer, n_tiles, n_lat, tpb, n_batch):
    r, d = h.shape
    depth = w_in.shape[0]
    mod_map, rope_map = _row_maps(layer, n_lat, tpb, n_batch)
    heads = (A_HEADS, B_HEADS, C_HEADS, A_KV, B_KV, C_HEADS, A_KV, B_KV, C_HEADS)
    transposed = tuple(j >= 6 for j in range(9))
    return pl.pallas_call(
        _qkv_kernel,
        grid=(n_tiles,),
        in_specs=[
            pl.BlockSpec((TM, d), lambda i: (i, 0)),
            pl.BlockSpec((None, 1, N_MOD, d), mod_map),
            _resident((depth, 1, d), layer),
            _resident(w_in.shape, layer),
            _resident((1, LANES)),
            _resident((1, LANES)),
            pl.BlockSpec((TM, LANES), rope_map),
            pl.BlockSpec((TM, LANES), rope_map),
            _resident((LANES, LANES)),
        ],
        out_specs=[pl.BlockSpec((n, VT_ROWS, TM), lambda i: (0, 0, i)) if t else
                   pl.BlockSpec((n, TM, HEAD_DIM), lambda i: (0, i, 0)) for n, t in zip(heads, transposed)],
        out_shape=[jax.ShapeDtypeStruct((n, VT_ROWS, r) if t else (n, r, HEAD_DIM), BF16)
                   for n, t in zip(heads, transposed)],
        compiler_params=_cparams(1),
        name="qkv",
    )(h, mod, gain.reshape(depth, 1, d), w_in, jnp.tile(q_gain, 2).reshape(1, LANES),
      jnp.tile(k_gain, 2).reshape(1, LANES), cosf, sinf, seg)


def _softmax_pv_t(parts, extra_logit=None):
    m = functools.reduce(jnp.maximum, [jnp.max(s, axis=0, keepdims=True) for s, _ in parts])
    if extra_logit is not None:
        m = jnp.maximum(m, extra_logit)
    acc = None
    for s, vt in parts:
        pv = _dot(vt, jnp.exp2(s - m).astype(BF16))
        acc = pv if acc is None else acc + pv
    den = acc[HEAD_DIM:HEAD_DIM + 1]
    if extra_logit is not None:
        den = den + jnp.exp2(extra_logit - m)
    return acc[:HEAD_DIM] / den


def _win_kernel(sink_ref, q_ref, k_ref, vt_ref, kc_ref, vtc_ref, o_ref, s_scr, mask_scr, *, seq):
    tq = TQ_WIN
    n_t = q_ref.shape[1] // tq
    grp = A_HEADS // A_KV
    kb = tq + 2 * A_WINDOW
    tile0 = pl.program_id(1) * n_t
    last_tile = seq // tq - 1
    sinks = [jnp.concatenate([jnp.full((1, tq), sink_ref[0, kv * grp + g] * LOG2_E, F32)
                              for g in range(grp)], axis=1) for kv in range(A_KV)]

    @pl.when((pl.program_id(0) == 0) & (pl.program_id(1) == 0))
    def _():
        rel = (lax.broadcasted_iota(jnp.int32, (kb, tq), 0) - lax.broadcasted_iota(jnp.int32, (kb, tq), 1))
        for cls in range(3):
            ok = jnp.abs(rel - cls * A_WINDOW) <= A_WINDOW
            mask_scr[cls] = jnp.concatenate([jnp.where(ok, 0.0, NEG)] * grp, axis=1)

    def band(t):
        g = tile0 + t
        q0 = g * tq
        cls = jnp.where(g == 0, 0, jnp.where(g == last_tile, 2, 1))
        return cls, pl.multiple_of(jnp.clip(q0 - A_WINDOW, 0, seq - kb), A_WINDOW)

    def scores(slot, t):
        cls, ks = band(t)
        for kv in range(A_KV):
            q = q_ref[kv * grp:(kv + 1) * grp, pl.ds(pl.multiple_of(t * tq, tq), tq), :]
            q = q.reshape(grp * tq, HEAD_DIM)
            s_scr[slot, kv, :kb, :] = _dot_nt(k_ref[kv, pl.ds(ks, kb), :], q) + mask_scr[cls]
            s_scr[slot, kv, kb:, :] = _dot_nt(kc_ref[kv], q)

    def absorb(slot, t):
        _, ks = band(t)
        accs, sink_w = [], []
        for kv in range(A_KV):
            s = s_scr[slot, kv]
            m = jnp.maximum(jnp.max(s, axis=0, keepdims=True), sinks[kv])
            e = jnp.exp2(s - m).astype(BF16)
            accs.append(_dot(vt_ref[kv, :, pl.ds(ks, kb)], e[:kb]) + _dot(vtc_ref[kv], e[kb:]))
            sink_w.append(jnp.exp2(sinks[kv] - m))
        return jnp.concatenate(accs, axis=0), jnp.concatenate(sink_w, axis=0)

    def store(t, stage):
        acc, sink_w = stage
        outs = []
        for kv in range(A_KV):
            a = acc[kv * VT_ROWS:(kv + 1) * VT_ROWS]
            o = a[:HEAD_DIM] / (a[HEAD_DIM:HEAD_DIM + 1] + sink_w[kv:kv + 1])
            outs.extend(o[:, g * tq:(g + 1) * tq] for g in range(grp))
        o_ref[pl.ds(pl.multiple_of(t * tq, tq), tq), :] = jnp.concatenate(outs, axis=0).T.astype(o_ref.dtype)

    scores(0, 0)

    def body(j, prev):
        for u in range(WIN_UNROLL):
            t = WIN_UNROLL * j + u
            scores((u + 1) % 2, jnp.minimum(t + 1, n_t - 1))
            cur = absorb(u % 2, t)
            store(jnp.maximum(t - 1, 0), prev)
            prev = cur
        return prev

    init = (jnp.ones((A_KV * VT_ROWS, grp * tq), F32), jnp.zeros((A_KV, grp * tq), F32))
    store(n_t - 1, lax.fori_loop(0, n_t // WIN_UNROLL, body, init))


def _attn_window(q, k, vt, sink, *, n_batch, seq, ctx_len):
    r = n_batch * seq
    n_t = min(WIN_TILES, seq // TQ_WIN)
    tb = n_t * TQ_WIN
    nq = seq // tb
    cb = n_batch * seq // ctx_len
    return pl.pallas_call(
        functools.partial(_win_kernel, seq=seq),
        grid=(n_batch, nq),
        in_specs=[
            pl.BlockSpec(memory_space=pltpu.SMEM),
            pl.BlockSpec((A_HEADS, tb, HEAD_DIM), lambda b, i: (0, b * nq + i, 0)),
            pl.BlockSpec((A_KV, seq, HEAD_DIM), lambda b, i: (0, b, 0)),
            pl.BlockSpec((A_KV, VT_ROWS, seq), lambda b, i: (0, 0, b)),
            pl.BlockSpec((A_KV, ctx_len, HEAD_DIM), lambda b, i: (0, cb + b, 0)),
            pl.BlockSpec((A_KV, VT_ROWS, ctx_len), lambda b, i: (0, 0, cb + b)),
        ],
        out_specs=pl.BlockSpec((tb, A_HEADS * HEAD_DIM), lambda b, i: (b * nq + i, 0)),
        out_shape=jax.ShapeDtypeStruct((r, A_HEADS * HEAD_DIM), BF16),
        scratch_shapes=[pltpu.VMEM((2, A_KV, TQ_WIN + 2 * A_WINDOW + ctx_len, (A_HEADS // A_KV) * TQ_WIN), F32),
                        pltpu.VMEM((3, TQ_WIN + 2 * A_WINDOW, (A_HEADS // A_KV) * TQ_WIN), F32)],
        compiler_params=_cparams(2),
        name="attn_window",
    )(sink.reshape(1, A_HEADS), q, k, vt, k, vt)


def _glob_kernel(bound_ref, q_ref, k_ref, vt_ref, kc_ref, vtc_ref, o_ref, s_scr, *, seq):
    tq = q_ref.shape[1]
    grp = B_HEADS // B_KV
    n_chunks = seq // TK_GLOB
    bound = bound_ref[0, 0]

    def queries(kv):
        return q_ref[kv * grp:(kv + 1) * grp].reshape(grp * tq, HEAD_DIM)

    def scores(c, kv):
        off = pl.multiple_of(c * TK_GLOB, TK_GLOB)
        return _dot_nt(k_ref[kv, pl.ds(off, TK_GLOB), :], queries(kv))

    def vt_chunk(c, kv):
        return vt_ref[kv, :, pl.ds(pl.multiple_of(c * TK_GLOB, TK_GLOB), TK_GLOB)]

    def finish(accs):
        outs = []
        for acc in accs:
            o = acc[:HEAD_DIM] / acc[HEAD_DIM:HEAD_DIM + 1]
            outs.extend(o[:, g * tq:(g + 1) * tq] for g in range(grp))
        o_ref[...] = jnp.concatenate(outs, axis=0).T.astype(o_ref.dtype)

    @pl.when(bound <= GLOB_SAFE_BOUND)
    def _():
        def weights(s):
            return jnp.exp2(s - bound).astype(BF16)

        acc0 = tuple(_dot(vtc_ref[kv], weights(_dot_nt(kc_ref[kv], queries(kv)))) for kv in range(B_KV))

        def body(c, accs):
            return tuple(acc + _dot(vt_chunk(c, kv), weights(scores(c, kv))) for kv, acc in enumerate(accs))

        finish(lax.fori_loop(0, n_chunks, body, acc0, unroll=4))

    @pl.when(bound > GLOB_SAFE_BOUND)
    def _():
        def absorb(s, vt, m, acc):
            m_new = jnp.maximum(m, jnp.max(s, axis=0, keepdims=True))
            e = jnp.exp2(s - m_new).astype(BF16)
            return m_new, jnp.exp2(m - m_new) * acc + _dot(vt, e)

        accs = []
        for kv in range(B_KV):
            s = _dot_nt(kc_ref[kv], queries(kv))
            m0 = jnp.max(s, axis=0, keepdims=True)
            a0 = _dot(vtc_ref[kv], jnp.exp2(s - m0).astype(BF16))
            s_scr[0] = scores(0, kv)

            def body(j, carry, kv=kv):
                m, acc = carry
                s_scr[1] = scores(2 * j + 1, kv)
                m, acc = absorb(s_scr[0], vt_chunk(2 * j, kv), m, acc)
                s_scr[0] = scores(jnp.minimum(2 * j + 2, n_chunks - 1), kv)
                return absorb(s_scr[1], vt_chunk(2 * j + 1, kv), m, acc)

            accs.append(lax.fori_loop(0, n_chunks // 2, body, (m0, a0))[1])
        finish(accs)


def _logit_bound(q_gain, k_gain):
    bound = (HEAD_DIM * QK_SCALE * LOG2_E * 1.02) * jnp.max(jnp.abs(q_gain)) * jnp.max(jnp.abs(k_gain))
    return bound.astype(F32).reshape(1, 1)


def _attn_global(q, k, vt, bound, *, n_batch, seq, ctx_len):
    r = n_batch * seq
    tq = TQ_GLOB
    nq = seq // tq
    cb = n_batch * seq // ctx_len
    return pl.pallas_call(
        functools.partial(_glob_kernel, seq=seq),
        grid=(n_batch, nq),
        in_specs=[
            pl.BlockSpec(memory_space=pltpu.SMEM),
            pl.BlockSpec((B_HEADS, tq, HEAD_DIM), lambda b, i: (0, b * nq + i, 0)),
            pl.BlockSpec((B_KV, seq, HEAD_DIM), lambda b, i: (0, b, 0)),
            pl.BlockSpec((B_KV, VT_ROWS, seq), lambda b, i: (0, 0, b)),
            pl.BlockSpec((B_KV, ctx_len, HEAD_DIM), lambda b, i: (0, cb + b, 0)),
            pl.BlockSpec((B_KV, VT_ROWS, ctx_len), lambda b, i: (0, 0, cb + b)),
        ],
        out_specs=pl.BlockSpec((tq, B_HEADS * HEAD_DIM), lambda b, i: (b * nq + i, 0)),
        out_shape=jax.ShapeDtypeStruct((r, B_HEADS * HEAD_DIM), BF16),
        scratch_shapes=[pltpu.VMEM((2, TK_GLOB, (B_HEADS // B_KV) * tq), F32)],
        compiler_params=_cparams(2),
        name="attn_global",
    )(bound, q, k, vt, k, vt)


def _nbr_kernel(q_ref, k_ref, vt_ref, kc_ref, vtc_ref, bias_ref, o_ref, s_scr, *, rows):
    tq = NBR_ROWS * GRID_W
    kk = NBR_KROWS * GRID_W
    n_t = q_ref.shape[1] // tq
    tile0 = pl.program_id(2) * n_t
    last_tile = rows // NBR_ROWS - 1

    def band(t):
        g = tile0 + t
        cls = jnp.where(g == 0, 0, jnp.where(g == last_tile, 2, 1))
        kr0 = jnp.clip(g * NBR_ROWS - NA_KH // 2, 0, rows - NBR_KROWS)
        return cls, pl.multiple_of(kr0 * GRID_W, LANES)

    def scores(slot, t):
        cls, ks = band(t)
        for h in range(2):
            q = q_ref[h, pl.ds(pl.multiple_of(t * tq, tq), tq), :]
            s_scr[slot, h, :kk, :] = _dot_nt(k_ref[h, pl.ds(ks, kk), :], q) + bias_ref[cls, h]
            s_scr[slot, h, kk:, :] = _dot_nt(kc_ref[h], q)

    def absorb(slot, t):
        _, ks = band(t)
        accs = []
        for h in range(2):
            s = s_scr[slot, h]
            e = jnp.exp2(s - jnp.max(s, axis=0, keepdims=True)).astype(BF16)
            accs.append(_dot(vt_ref[h, :, pl.ds(ks, kk)], e[:kk]) + _dot(vtc_ref[h], e[kk:]))
        return jnp.concatenate(accs, axis=0)

    def store(t, acc):
        o = [acc[h * VT_ROWS:h * VT_ROWS + HEAD_DIM] / acc[h * VT_ROWS + HEAD_DIM:h * VT_ROWS + HEAD_DIM + 1]
             for h in range(2)]
        o_ref[pl.ds(pl.multiple_of(t * tq, tq), tq), :] = jnp.concatenate(o, axis=0).T.astype(o_ref.dtype)

    scores(0, 0)

    def body(j, prev):
        for u in range(NBR_UNROLL):
            t = NBR_UNROLL * j + u
            scores((u + 1) % 2, jnp.minimum(t + 1, n_t - 1))
            cur = absorb(u % 2, t)
            store(jnp.maximum(t - 1, 0), prev)
            prev = cur
        return prev

    store(n_t - 1, lax.fori_loop(0, n_t // NBR_UNROLL, body, jnp.ones((2 * VT_ROWS, tq), F32)))


def _nbr_bias_kernel(rpb_ref, o_ref, *, rows):
    h = pl.program_id(0)
    n_dc = 2 * NA_KW - 1
    kcol = lax.broadcasted_iota(jnp.int32, (GRID_W, GRID_W), 0)
    col = lax.broadcasted_iota(jnp.int32, (GRID_W, GRID_W), 1)
    dc = kcol - col + NA_KW - 1
    cs = jnp.clip(col - NA_KW // 2, 0, GRID_W - NA_KW)
    col_ok = (kcol >= cs) & (kcol < cs + NA_KW)
    neg = jnp.full((GRID_W, GRID_W), NEG, F32)
    by_dr = []
    for dr in range(2 * NA_KH - 1):
        blk = neg
        for j in range(n_dc):
            blk = jnp.where(dc == j, rpb_ref[h, dr * n_dc + j] * LOG2_E, blk)
        by_dr.append(jnp.where(col_ok, blk, NEG))
    for cls, r0 in enumerate((0, NBR_ROWS, rows - NBR_ROWS)):
        kr0 = min(max(r0 - NA_KH // 2, 0), rows - NBR_KROWS)
        for kri in range(NBR_KROWS):
            krow = kr0 + kri
            blocks = []
            for ri in range(NBR_ROWS):
                rs = min(max(r0 + ri - NA_KH // 2, 0), rows - NA_KH)
                blocks.append(by_dr[krow - r0 - ri + NA_KH - 1] if rs <= krow < rs + NA_KH else neg)
            o_ref[cls, 0, kri * GRID_W:(kri + 1) * GRID_W, :] = jnp.concatenate(blocks, axis=-1)


def _nbr_bias(rpb, rows):
    n_heads = rpb.shape[0]
    tq, kk = NBR_ROWS * GRID_W, NBR_KROWS * GRID_W
    return pl.pallas_call(
        functools.partial(_nbr_bias_kernel, rows=rows),
        grid=(n_heads,),
        in_specs=[pl.BlockSpec(memory_space=pltpu.SMEM)],
        out_specs=pl.BlockSpec((3, 1, kk, tq), lambda h: (0, h, 0, 0)),
        out_shape=jax.ShapeDtypeStruct((3, n_heads, kk, tq), F32),
        compiler_params=_cparams(1),
        name="nbr_bias",
    )(rpb.astype(F32).reshape(n_heads, -1))


def _attn_nbr(q, k, vt, bias, *, n_batch, seq, ctx_len):
    r = n_batch * seq
    rows = seq // GRID_W
    tq = NBR_ROWS * GRID_W
    kk = NBR_KROWS * GRID_W
    tb = min(NBR_TILES, seq // tq) * tq
    nq = seq // tb
    cb = n_batch * seq // ctx_len
    return pl.pallas_call(
        functools.partial(_nbr_kernel, rows=rows),
        grid=(n_batch, C_HEADS // 2, nq),
        in_specs=[
            pl.BlockSpec((2, tb, HEAD_DIM), lambda b, hp, i: (hp, b * nq + i, 0)),
            pl.BlockSpec((2, seq, HEAD_DIM), lambda b, hp, i: (hp, b, 0)),
            pl.BlockSpec((2, VT_ROWS, seq), lambda b, hp, i: (hp, 0, b)),
            pl.BlockSpec((2, ctx_len, HEAD_DIM), lambda b, hp, i: (hp, cb + b, 0)),
            pl.BlockSpec((2, VT_ROWS, ctx_len), lambda b, hp, i: (hp, 0, cb + b)),
            pl.BlockSpec((3, 2, kk, tq), lambda b, hp, i: (0, hp, 0, 0)),
        ],
        out_specs=pl.BlockSpec((tb, 2 * HEAD_DIM), lambda b, hp, i: (b * nq + i, hp)),
        out_shape=jax.ShapeDtypeStruct((r, C_HEADS * HEAD_DIM), BF16),
        scratch_shapes=[pltpu.VMEM((2, 2, kk + ctx_len, tq), F32)],
        compiler_params=_cparams(3),
        name="attn_nbr",
    )(q, k, vt, k, vt, bias)


def _ctx_kernel(sink_ref, qw_ref, qg_ref, qn_ref, kw_ref, kg_ref, kn_ref, vw_ref, vg_ref, vn_ref,
                yw_o, yg_o, yn_o):
    def group(q_ref, k_ref, vt_ref, o_ref, n_heads, n_kv, sink=False):
        outs = []
        for h in range(n_heads):
            kv = h // (n_heads // n_kv)
            s = _dot_nt(k_ref[kv], q_ref[h])
            outs.append(_softmax_pv_t([(s, vt_ref[kv])],
                                      extra_logit=sink_ref[0, h] * LOG2_E if sink else None))
        o_ref[...] = jnp.concatenate(outs, axis=0).T.astype(o_ref.dtype)

    group(qw_ref, kw_ref, vw_ref, yw_o, A_HEADS, A_KV, sink=True)
    group(qg_ref, kg_ref, vg_ref, yg_o, B_HEADS, B_KV)
    group(qn_ref, kn_ref, vn_ref, yn_o, C_HEADS, C_HEADS)


def _attn_ctx(sink, qs, ks, vs, *, n_batch, seq, ctx_len):
    cb = n_batch * seq // ctx_len

    def hspec(a):
        if a.shape[1] == VT_ROWS:
            return pl.BlockSpec((a.shape[0], VT_ROWS, ctx_len), lambda b: (0, 0, cb + b))
        return pl.BlockSpec((a.shape[0], ctx_len, HEAD_DIM), lambda b: (0, cb + b, 0))

    widths = [q.shape[0] * HEAD_DIM for q in qs]
    return pl.pallas_call(
        _ctx_kernel,
        grid=(n_batch,),
        in_specs=[pl.BlockSpec(memory_space=pltpu.SMEM)] + [hspec(a) for a in (*qs, *ks, *vs)],
        out_specs=[pl.BlockSpec((ctx_len, w), lambda b: (b, 0)) for w in widths],
        out_shape=[jax.ShapeDtypeStruct((n_batch * ctx_len, w), BF16) for w in widths],
        compiler_params=_cparams(1),
        name="attn_ctx",
    )(sink.reshape(1, A_HEADS), *qs, *ks, *vs)


def _rope_tables(seq):
    t = jnp.arange(seq)
    row = (t // GRID_W).astype(F32)
    col = (t % GRID_W).astype(F32)
    n_freq = HEAD_DIM // 4
    inv = ROPE_THETA ** (-jnp.arange(n_freq, dtype=F32) / n_freq)
    ang = jnp.concatenate([row[:, None] * inv, col[:, None] * inv], axis=-1)
    cos, sin = jnp.cos(ang), jnp.sin(ang)
    cosf = jnp.concatenate([jnp.tile(cos, (1, 4)), jnp.ones((TM, LANES), F32)], axis=0)
    sinf = jnp.concatenate([jnp.tile(jnp.concatenate([-sin, sin], axis=-1), (1, 2)),
                            jnp.zeros((TM, LANES), F32)], axis=0)
    return cosf, sinf


def kernel(x, c, ctx, c_ctx, w_ada, b_ada, norm_ffn1, w_ffn1_gate, w_ffn1_up, w_ffn1_down,
           norm_mix, w_in, q_norm_glob, k_norm_glob, sink_win, rpb_nbr, w_out,
           norm_ffn2, w_ffn2_gate, w_ffn2_up, w_ffn2_down, norm_final):
    n_batch, seq, d = x.shape
    ctx_len = ctx.shape[1]
    depth = w_ada.shape[0]
    r_lat, r_ctx = n_batch * seq, n_batch * ctx_len
    assert seq % TM == 0 and r_ctx % TM == 0 and n_batch + 1 <= MOD_ROWS
    assert seq % (min(WIN_TILES, seq // TQ_WIN) * TQ_WIN) == 0
    assert seq % TQ_GLOB == 0 and seq % (2 * TK_GLOB) == 0
    assert seq % (NBR_ROWS * GRID_W) == 0 and r_lat % ctx_len == 0 and seq >= TQ_WIN + 2 * A_WINDOW
    n_lat, n_all, tpb = r_lat // TM, (r_lat + r_ctx) // TM, seq // TM
    tiles = dict(n_lat=n_lat, tpb=tpb, n_batch=n_batch)
    dims = dict(n_batch=n_batch, seq=seq, ctx_len=ctx_len)

    h_srcs = (x.reshape(r_lat, d), ctx.reshape(r_ctx, d))
    cc = jnp.zeros((MOD_ROWS, d), F32).at[:n_batch].set(c).at[n_batch].set(c_ctx)
    mod = _ada(cc, w_ada, b_ada).reshape(depth, MOD_ROWS, N_MOD, d)
    cosf, sinf = _rope_tables(seq)
    seg = jnp.asarray(np.kron(np.eye(2), np.full((HEAD_DIM, HEAD_DIM), 1.0 / HEAD_DIM)), BF16)

    ffn1 = [w.astype(BF16) for w in (w_ffn1_gate, w_ffn1_up, w_ffn1_down)]
    ffn2 = [w.astype(BF16) for w in (w_ffn2_gate, w_ffn2_up, w_ffn2_down)]
    w_in_b, w_out_b = w_in.astype(BF16), w_out.astype(BF16)

    for l in range(depth):
        last = l == depth - 1
        h = _ffn(h_srcs, mod, norm_ffn1, *ffn1, norm_final, layer=l, k0=0, n_tiles=n_all, final=False, **tiles)
        qw, qg, qn, kw, kg, kn, vw, vg, vn = _qkv(
            h, mod, norm_mix, w_in_b, q_norm_glob[l], k_norm_glob[l],
            cosf, sinf, seg, layer=l, n_tiles=n_all, **tiles)
        ys = [(_attn_window(qw, kw, vw, sink_win[l], **dims),),
              (_attn_global(qg, kg, vg, _logit_bound(q_norm_glob[l], k_norm_glob[l]), **dims),),
              (_attn_nbr(qn, kn, vn, _nbr_bias(rpb_nbr[l], seq // GRID_W), **dims),)]
        if not last:
            ycs = _attn_ctx(sink_win[l], (qw, qg, qn), (kw, kg, kn), (vw, vg, vn), **dims)
            ys = [(y, yc) for (y,), yc in zip(ys, ycs)]
        h = _ffn((h,), mod, norm_ffn2, *ffn2, norm_final, layer=l, k0=6, n_tiles=n_lat if last else n_all,
                 final=last, y_srcs=ys, w_out=w_out_b, **tiles)
        h_srcs = (h,)
    return h.reshape(n_batch, seq, d)
```

```python
import functools

import numpy as np
import jax
import jax.numpy as jnp
from jax import lax
from jax.experimental import pallas as pl
from jax.experimental.pallas import tpu as pltpu

D_MODEL = 1024
HEAD_DIM = 64
GRID_W = 64
A_HEADS, A_KV, A_WINDOW = 6, 2, 128
B_HEADS, B_KV = 4, 2
C_HEADS = 6
NA_KH, NA_KW = 8, 16
ROPE_THETA = 10000.0
EPS = 1e-6
N_MOD = 9
NEG = -1e30
MIX_WIDTH = D_MODEL
QK_SCALE = HEAD_DIM ** -0.5
LOG2_E = 1.4426950408889634
VT_ROWS = HEAD_DIM + 16
GLOB_SAFE_BOUND = 40.0

W_IN_HEADS = dict(qw=A_HEADS, qg=B_HEADS, qn=C_HEADS, kw=A_KV, vw=A_KV, kg=B_KV, vg=B_KV, kn=C_HEADS, vn=C_HEADS)

LANES = 128
MOD_ROWS = 8
TM = 512
TQ_WIN = 256
WIN_TILES = 16
WIN_UNROLL = 2
TQ_GLOB = 1024
TK_GLOB = 512
NBR_ROWS = 4
NBR_TILES = 16
NBR_UNROLL = 4
NBR_KROWS = NBR_ROWS + NA_KH
VMEM_LIMIT = 56 * 1024 * 1024

F32 = jnp.float32
BF16 = jnp.bfloat16


def _cparams(n_axes):
    return pltpu.CompilerParams(dimension_semantics=("arbitrary",) * n_axes,
                                vmem_limit_bytes=VMEM_LIMIT)


def _dot(a, b):
    return jnp.dot(a, b, preferred_element_type=F32)


def _dot_nt(a, b):
    return lax.dot_general(a, b, (((1,), (1,)), ((), ())), preferred_element_type=F32)


def _rms(x, g):
    return x * lax.rsqrt(jnp.mean(x * x, axis=-1, keepdims=True) + EPS) * g


def _ada_kernel(cc_ref, w_ref, b_ref, o_ref):
    a = cc_ref[...]
    a = a * jax.nn.sigmoid(a)
    o_ref[0] = jnp.dot(a, w_ref[0], preferred_element_type=F32,
                       precision=lax.Precision.HIGHEST) + b_ref[0]


def _ada(cc, w_ada, b_ada):
    depth, d, n = w_ada.shape
    tn = 1024
    return pl.pallas_call(
        _ada_kernel,
        grid=(depth, n // tn),
        in_specs=[
            pl.BlockSpec((MOD_ROWS, d), lambda l, j: (0, 0)),
            pl.BlockSpec((1, d, tn), lambda l, j: (l, 0, j)),
            pl.BlockSpec((1, 1, tn), lambda l, j: (l, 0, j)),
        ],
        out_specs=pl.BlockSpec((1, MOD_ROWS, tn), lambda l, j: (l, 0, j)),
        out_shape=jax.ShapeDtypeStruct((depth, MOD_ROWS, n), F32),
        compiler_params=_cparams(2),
        name="ada",
    )(cc, w_ada, b_ada.reshape(depth, 1, n))


def _pick_tile(refs, n_lat):
    if len(refs) == 1:
        return refs[0][...]
    return jnp.where(pl.program_id(0) < n_lat, refs[0][...], refs[1][...])


def _ffn_kernel(*refs, k0, final, n_h, n_y, n_lat):
    refs = list(refs)
    h_refs = [refs.pop(0) for _ in range(n_h)]
    y_refs = [[refs.pop(0) for _ in range(n_y)] for _ in range(3 if n_y else 0)]
    mod_ref, g_ref, wg_ref, wu_ref, wd_ref, gf_ref = (refs.pop(0) for _ in range(6))
    wo_ref = refs.pop(0) if n_y else None
    (o_ref,) = refs
    h = _pick_tile(h_refs, n_lat)
    mod = mod_ref[0]
    if n_y:
        y = jnp.concatenate([_pick_tile(yr, n_lat) for yr in y_refs], axis=1)
        h = h + mod[5:6] * _dot(y, wo_ref[...])
    u = _rms(h, g_ref[...]) * (1.0 + mod[k0 + 1:k0 + 2]) + mod[k0:k0 + 1]
    ub = u.astype(BF16)
    gate = _dot(ub, wg_ref[...])
    up = _dot(ub, wu_ref[...])
    a = (gate * jax.nn.sigmoid(gate) * up).astype(BF16)
    y = _dot(a, wd_ref[...])
    out = h + 0.5 * mod[k0 + 2:k0 + 3] * y
    if final:
        out = _rms(out, gf_ref[...])
    o_ref[...] = out


def _row_maps(layer, n_lat, tpb, n_batch):
    def mod_map(i):
        return (layer, jnp.where(i < n_lat, i // tpb, n_batch), 0, 0)

    def rope_map(i):
        return (jnp.where(i < n_lat, i % tpb, tpb), 0)

    return mod_map, rope_map


def _resident(shape, layer=None):
    nd = len(shape)
    if layer is None:
        return pl.BlockSpec(shape, lambda i: (0,) * nd, pipeline_mode=pl.Buffered(1))
    return pl.BlockSpec((None, *shape[1:]), lambda i: (layer,) + (0,) * (nd - 1), pipeline_mode=pl.Buffered(1))


def _tile_specs(arrays, n_lat):
    if len(arrays) == 1:
        return [pl.BlockSpec((TM, arrays[0].shape[1]), lambda i: (i, 0))]
    return [pl.BlockSpec((TM, arrays[0].shape[1]), lambda i: (jnp.minimum(i, n_lat - 1), 0)),
            pl.BlockSpec((TM, arrays[1].shape[1]), lambda i: (jnp.maximum(i - n_lat, 0), 0))]


def _ffn(h_srcs, mod, gain, wg, wu, wd, gain_final, *, layer, k0, n_tiles, n_lat, tpb, n_batch, final,
         y_srcs=(), w_out=None):
    depth, d, f = wg.shape
    mod_map, _ = _row_maps(layer, n_lat, tpb, n_batch)
    n_y = len(y_srcs[0]) if y_srcs else 0
    streams = [h_srcs, *y_srcs]
    return pl.pallas_call(
        functools.partial(_ffn_kernel, k0=k0, final=final, n_h=len(h_srcs), n_y=n_y, n_lat=n_lat),
        grid=(n_tiles,),
        in_specs=[spec for s in streams for spec in _tile_specs(s, n_lat)] + [
            pl.BlockSpec((None, 1, N_MOD, d), mod_map),
            _resident((depth, 1, d), layer),
            _resident(wg.shape, layer),
            _resident(wu.shape, layer),
            _resident(wd.shape, layer),
            _resident((1, d)),
        ] + ([_resident(w_out.shape, layer)] if n_y else []),
        out_specs=pl.BlockSpec((TM, d), lambda i: (i, 0)),
        out_shape=jax.ShapeDtypeStruct((n_tiles * TM, d), F32),
        compiler_params=_cparams(1),
        name="ffn",
    )(*[a for s in streams for a in s], mod, gain.reshape(depth, 1, d), wg, wu, wd, gain_final.reshape(1, d),
      *([w_out] if n_y else []))


def _swap_halves(x):
    lane = lax.broadcasted_iota(jnp.int32, x.shape, 1)
    return jnp.where((lane % HEAD_DIM) < HEAD_DIM // 2,
                     pltpu.roll(x, LANES - HEAD_DIM // 2, 1),
                     pltpu.roll(x, HEAD_DIM // 2, 1))


def _head_mean_sq(x, seg):
    y = x * x
    hi = y.astype(BF16)
    lo = (y - hi.astype(F32)).astype(BF16)
    return _dot(hi, seg) + _dot(lo, seg)


def _qkv_kernel(h_ref, mod_ref, g_ref, w_ref, qg_ref, kg_ref, cos_ref, sin_ref, seg_ref,
                qw_o, qgl_o, qn_o, kw_o, kgl_o, kn_o, vw_o, vgl_o, vn_o):
    h = h_ref[...]
    mod = mod_ref[0]
    u = _rms(h, g_ref[...]) * (1.0 + mod[4:5]) + mod[3:4]
    p = _dot(u.astype(BF16), w_ref[...])
    cosf = cos_ref[...]
    sinf = sin_ref[...]
    seg = seg_ref[...]

    def emit(col, out_ref, n_heads, gain_ref, rope, scale, transpose=False):
        for c in range(n_heads // 2):
            xc = p[:, col + c * LANES: col + (c + 1) * LANES]
            if gain_ref is not None:
                xc = xc * lax.rsqrt(_head_mean_sq(xc, seg) + EPS) * gain_ref[...]
            if rope:
                xc = xc * cosf + _swap_halves(xc) * sinf
            if scale is not None:
                xc = xc * scale
            if transpose:
                xt = xc.T
                ones = jnp.ones((VT_ROWS - HEAD_DIM, xt.shape[1]), BF16)
                for j in range(2):
                    out_ref[2 * c + j, :HEAD_DIM, :] = xt[j * HEAD_DIM:(j + 1) * HEAD_DIM].astype(BF16)
                    out_ref[2 * c + j, HEAD_DIM:, :] = ones
            else:
                out_ref[2 * c] = xc[:, :HEAD_DIM].astype(BF16)
                out_ref[2 * c + 1] = xc[:, HEAD_DIM:].astype(BF16)

    q_scale = QK_SCALE * LOG2_E
    plan = {
        "qw": (qw_o, None, True, q_scale, False), "qg": (qgl_o, qg_ref, True, q_scale, False),
        "qn": (qn_o, None, False, q_scale, False), "kw": (kw_o, None, True, None, False),
        "vw": (vw_o, None, False, None, True), "kg": (kgl_o, kg_ref, True, None, False),
        "vg": (vgl_o, None, False, None, True), "kn": (kn_o, None, False, None, False),
        "vn": (vn_o, None, False, None, True),
    }
    col = 0
    for name, n_heads in W_IN_HEADS.items():
        out_ref, gain_ref, rope, scale, transpose = plan[name]
        emit(col, out_ref, n_heads, gain_ref, rope, scale, transpose)
        col += n_heads * HEAD_DIM


def _qkv(h, mod, gain, w_in, q_gain, k_gain, cosf, sinf, seg, *, layer, n_tiles, n_lat, tpb, n_batch):
    r, d = h.shape
    depth = w_in.shape[0]
    mod_map, rope_map = _row_maps(layer, n_lat, tpb, n_batch)
    heads = (A_HEADS, B_HEADS, C_HEADS, A_KV, B_KV, C_HEADS, A_KV, B_KV, C_HEADS)
    transposed = tuple(j >= 6 for j in range(9))
    return pl.pallas_call(
        _qkv_kernel,
        grid=(n_tiles,),
        in_specs=[
            pl.BlockSpec((TM, d), lambda i: (i, 0)),
            pl.BlockSpec((None, 1, N_MOD, d), mod_map),
            _resident((depth, 1, d), layer),
            _resident(w_in.shape, layer),
            _resident((1, LANES)),
            _resident((1, LANES)),
            pl.BlockSpec((TM, LANES), rope_map),
            pl.BlockSpec((TM, LANES), rope_map),
            _resident((LANES, LANES)),
        ],
        out_specs=[pl.BlockSpec((n, VT_ROWS, TM), lambda i: (0, 0, i)) if t else
                   pl.BlockSpec((n, TM, HEAD_DIM), lambda i: (0, i, 0)) for n, t in zip(heads, transposed)],
        out_shape=[jax.ShapeDtypeStruct((n, VT_ROWS, r) if t else (n, r, HEAD_DIM), BF16)
                   for n, t in zip(heads, transposed)],
        compiler_params=_cparams(1),
        name="qkv",
    )(h, mod, gain.reshape(depth, 1, d), w_in, jnp.tile(q_gain, 2).reshape(1, LANES),
      jnp.tile(k_gain, 2).reshape(1, LANES), cosf, sinf, seg)


def _softmax_pv_t(parts, extra_logit=None):
    m = functools.reduce(jnp.maximum, [jnp.max(s, axis=0, keepdims=True) for s, _ in parts])
    if extra_logit is not None:
        m = jnp.maximum(m, extra_logit)
    acc = None
    for s, vt in parts:
        pv = _dot(vt, jnp.exp2(s - m).astype(BF16))
        acc = pv if acc is None else acc + pv
    den = acc[HEAD_DIM:HEAD_DIM + 1]
    if extra_logit is not None:
        den = den + jnp.exp2(extra_logit - m)
    return acc[:HEAD_DIM] / den


def _win_kernel(sink_ref, q_ref, k_ref, vt_ref, kc_ref, vtc_ref, o_ref, s_scr, mask_scr, *, seq):
    tq = TQ_WIN
    n_t = q_ref.shape[1] // tq
    grp = A_HEADS // A_KV
    kb = tq + 2 * A_WINDOW
    tile0 = pl.program_id(1) * n_t
    last_tile = seq // tq - 1
    sinks = [jnp.concatenate([jnp.full((1, tq), sink_ref[0, kv * grp + g] * LOG2_E, F32)
                              for g in range(grp)], axis=1) for kv in range(A_KV)]

    @pl.when((pl.program_id(0) == 0) & (pl.program_id(1) == 0))
    def _():
        rel = (lax.broadcasted_iota(jnp.int32, (kb, tq), 0) - lax.broadcasted_iota(jnp.int32, (kb, tq), 1))
        for cls in range(3):
            ok = jnp.abs(rel - cls * A_WINDOW) <= A_WINDOW
            mask_scr[cls] = jnp.concatenate([jnp.where(ok, 0.0, NEG)] * grp, axis=1)

    def band(t):
        g = tile0 + t
        q0 = g * tq
        cls = jnp.where(g == 0, 0, jnp.where(g == last_tile, 2, 1))
        return cls, pl.multiple_of(jnp.clip(q0 - A_WINDOW, 0, seq - kb), A_WINDOW)

    def scores(slot, t):
        cls, ks = band(t)
        for kv in range(A_KV):
            q = q_ref[kv * grp:(kv + 1) * grp, pl.ds(pl.multiple_of(t * tq, tq), tq), :]
            q = q.reshape(grp * tq, HEAD_DIM)
            s_scr[slot, kv, :kb, :] = _dot_nt(k_ref[kv, pl.ds(ks, kb), :], q) + mask_scr[cls]
            s_scr[slot, kv, kb:, :] = _dot_nt(kc_ref[kv], q)

    def absorb(slot, t):
        _, ks = band(t)
        accs, sink_w = [], []
        for kv in range(A_KV):
            s = s_scr[slot, kv]
            m = jnp.maximum(jnp.max(s, axis=0, keepdims=True), sinks[kv])
            e = jnp.exp2(s - m).astype(BF16)
            accs.append(_dot(vt_ref[kv, :, pl.ds(ks, kb)], e[:kb]) + _dot(vtc_ref[kv], e[kb:]))
            sink_w.append(jnp.exp2(sinks[kv] - m))
        return jnp.concatenate(accs, axis=0), jnp.concatenate(sink_w, axis=0)

    def store(t, stage):
        acc, sink_w = stage
        outs = []
        for kv in range(A_KV):
            a = acc[kv * VT_ROWS:(kv + 1) * VT_ROWS]
            o = a[:HEAD_DIM] / (a[HEAD_DIM:HEAD_DIM + 1] + sink_w[kv:kv + 1])
            outs.extend(o[:, g * tq:(g + 1) * tq] for g in range(grp))
        o_ref[pl.ds(pl.multiple_of(t * tq, tq), tq), :] = jnp.concatenate(outs, axis=0).T.astype(o_ref.dtype)

    scores(0, 0)

    def body(j, prev):
        for u in range(WIN_UNROLL):
            t = WIN_UNROLL * j + u
            scores((u + 1) % 2, jnp.minimum(t + 1, n_t - 1))
            cur = absorb(u % 2, t)
            store(jnp.maximum(t - 1, 0), prev)
            prev = cur
        return prev

    init = (jnp.ones((A_KV * VT_ROWS, grp * tq), F32), jnp.zeros((A_KV, grp * tq), F32))
    store(n_t - 1, lax.fori_loop(0, n_t // WIN_UNROLL, body, init))


def _attn_window(q, k, vt, sink, *, n_batch, seq, ctx_len):
    r = n_batch * seq
    n_t = min(WIN_TILES, seq // TQ_WIN)
    tb = n_t * TQ_WIN
    nq = seq // tb
    cb = n_batch * seq // ctx_len
    return pl.pallas_call(
        functools.partial(_win_kernel, seq=seq),
        grid=(n_batch, nq),
        in_specs=[
            pl.BlockSpec(memory_space=pltpu.SMEM),
            pl.BlockSpec((A_HEADS, tb, HEAD_DIM), lambda b, i: (0, b * nq + i, 0)),
            pl.BlockSpec((A_KV, seq, HEAD_DIM), lambda b, i: (0, b, 0)),
            pl.BlockSpec((A_KV, VT_ROWS, seq), lambda b, i: (0, 0, b)),
            pl.BlockSpec((A_KV, ctx_len, HEAD_DIM), lambda b, i: (0, cb + b, 0)),
            pl.BlockSpec((A_KV, VT_ROWS, ctx_len), lambda b, i: (0, 0, cb + b)),
        ],
        out_specs=pl.BlockSpec((tb, A_HEADS * HEAD_DIM), lambda b, i: (b * nq + i, 0)),
        out_shape=jax.ShapeDtypeStruct((r, A_HEADS * HEAD_DIM), BF16),
        scratch_shapes=[pltpu.VMEM((2, A_KV, TQ_WIN + 2 * A_WINDOW + ctx_len, (A_HEADS // A_KV) * TQ_WIN), F32),
                        pltpu.VMEM((3, TQ_WIN + 2 * A_WINDOW, (A_HEADS // A_KV) * TQ_WIN), F32)],
        compiler_params=_cparams(2),
        name="attn_window",
    )(sink.reshape(1, A_HEADS), q, k, vt, k, vt)


def _glob_kernel(bound_ref, q_ref, k_ref, vt_ref, kc_ref, vtc_ref, o_ref, s_scr, *, seq):
    tq = q_ref.shape[1]
    grp = B_HEADS // B_KV
    n_chunks = seq // TK_GLOB
    bound = bound_ref[0, 0]

    def queries(kv):
        return q_ref[kv * grp:(kv + 1) * grp].reshape(grp * tq, HEAD_DIM)

    def scores(c, kv):
        off = pl.multiple_of(c * TK_GLOB, TK_GLOB)
        return _dot_nt(k_ref[kv, pl.ds(off, TK_GLOB), :], queries(kv))

    def vt_chunk(c, kv):
        return vt_ref[kv, :, pl.ds(pl.multiple_of(c * TK_GLOB, TK_GLOB), TK_GLOB)]

    def finish(accs):
        outs = []
        for acc in accs:
            o = acc[:HEAD_DIM] / acc[HEAD_DIM:HEAD_DIM + 1]
            outs.extend(o[:, g * tq:(g + 1) * tq] for g in range(grp))
        o_ref[...] = jnp.concatenate(outs, axis=0).T.astype(o_ref.dtype)

    @pl.when(bound <= GLOB_SAFE_BOUND)
    def _():
        def weights(s):
            return jnp.exp2(s - bound).astype(BF16)

        acc0 = tuple(_dot(vtc_ref[kv], weights(_dot_nt(kc_ref[kv], queries(kv)))) for kv in range(B_KV))

        def body(c, accs):
            return tuple(acc + _dot(vt_chunk(c, kv), weights(scores(c, kv))) for kv, acc in enumerate(accs))

        finish(lax.fori_loop(0, n_chunks, body, acc0, unroll=8))

    @pl.when(bound > GLOB_SAFE_BOUND)
    def _():
        def absorb(s, vt, m, acc):
            m_new = jnp.maximum(m, jnp.max(s, axis=0, keepdims=True))
            e = jnp.exp2(s - m_new).astype(BF16)
            return m_new, jnp.exp2(m - m_new) * acc + _dot(vt, e)

        accs = []
        for kv in range(B_KV):
            s = _dot_nt(kc_ref[kv], queries(kv))
            m0 = jnp.max(s, axis=0, keepdims=True)
            a0 = _dot(vtc_ref[kv], jnp.exp2(s - m0).astype(BF16))
            s_scr[0] = scores(0, kv)

            def body(j, carry, kv=kv):
                m, acc = carry
                s_scr[1] = scores(2 * j + 1, kv)
                m, acc = absorb(s_scr[0], vt_chunk(2 * j, kv), m, acc)
                s_scr[0] = scores(jnp.minimum(2 * j + 2, n_chunks - 1), kv)
                return absorb(s_scr[1], vt_chunk(2 * j + 1, kv), m, acc)

            accs.append(lax.fori_loop(0, n_chunks // 2, body, (m0, a0))[1])
        finish(accs)


def _logit_bound(q_gain, k_gain):
    bound = (HEAD_DIM * QK_SCALE * LOG2_E * 1.02) * jnp.max(jnp.abs(q_gain)) * jnp.max(jnp.abs(k_gain))
    return bound.astype(F32).reshape(1, 1)


def _attn_global(q, k, vt, bound, *, n_batch, seq, ctx_len):
    r = n_batch * seq
    tq = TQ_GLOB
    nq = seq // tq
    cb = n_batch * seq // ctx_len
    return pl.pallas_call(
        functools.partial(_glob_kernel, seq=seq),
        grid=(n_batch, nq),
        in_specs=[
            pl.BlockSpec(memory_space=pltpu.SMEM),
            pl.BlockSpec((B_HEADS, tq, HEAD_DIM), lambda b, i: (0, b * nq + i, 0)),
            pl.BlockSpec((B_KV, seq, HEAD_DIM), lambda b, i: (0, b, 0)),
            pl.BlockSpec((B_KV, VT_ROWS, seq), lambda b, i: (0, 0, b)),
            pl.BlockSpec((B_KV, ctx_len, HEAD_DIM), lambda b, i: (0, cb + b, 0)),
            pl.BlockSpec((B_KV, VT_ROWS, ctx_len), lambda b, i: (0, 0, cb + b)),
        ],
        out_specs=pl.BlockSpec((tq, B_HEADS * HEAD_DIM), lambda b, i: (b * nq + i, 0)),
        out_shape=jax.ShapeDtypeStruct((r, B_HEADS * HEAD_DIM), BF16),
        scratch_shapes=[pltpu.VMEM((2, TK_GLOB, (B_HEADS // B_KV) * tq), F32)],
        compiler_params=_cparams(2),
        name="attn_global",
    )(bound, q, k, vt, k, vt)


def _nbr_kernel(q_ref, k_ref, vt_ref, kc_ref, vtc_ref, bias_ref, o_ref, s_scr, *, rows):
    tq = NBR_ROWS * GRID_W
    kk = NBR_KROWS * GRID_W
    n_t = q_ref.shape[1] // tq
    tile0 = pl.program_id(2) * n_t
    last_tile = rows // NBR_ROWS - 1

    def band(t):
        g = tile0 + t
        cls = jnp.where(g == 0, 0, jnp.where(g == last_tile, 2, 1))
        kr0 = jnp.clip(g * NBR_ROWS - NA_KH // 2, 0, rows - NBR_KROWS)
        return cls, pl.multiple_of(kr0 * GRID_W, LANES)

    def scores(slot, t):
        cls, ks = band(t)
        for h in range(2):
            q = q_ref[h, pl.ds(pl.multiple_of(t * tq, tq), tq), :]
            s_scr[slot, h, :kk, :] = _dot_nt(k_ref[h, pl.ds(ks, kk), :], q) + bias_ref[cls, h]
            s_scr[slot, h, kk:, :] = _dot_nt(kc_ref[h], q)

    def absorb(slot, t):
        _, ks = band(t)
        accs = []
        for h in range(2):
            s = s_scr[slot, h]
            e = jnp.exp2(s - jnp.max(s, axis=0, keepdims=True)).astype(BF16)
            accs.append(_dot(vt_ref[h, :, pl.ds(ks, kk)], e[:kk]) + _dot(vtc_ref[h], e[kk:]))
        return jnp.concatenate(accs, axis=0)

    def store(t, acc):
        o = [acc[h * VT_ROWS:h * VT_ROWS + HEAD_DIM] / acc[h * VT_ROWS + HEAD_DIM:h * VT_ROWS + HEAD_DIM + 1]
             for h in range(2)]
        o_ref[pl.ds(pl.multiple_of(t * tq, tq), tq), :] = jnp.concatenate(o, axis=0).T.astype(o_ref.dtype)

    scores(0, 0)

    def body(j, prev):
        for u in range(NBR_UNROLL):
            t = NBR_UNROLL * j + u
            scores((u + 1) % 2, jnp.minimum(t + 1, n_t - 1))
            cur = absorb(u % 2, t)
            store(jnp.maximum(t - 1, 0), prev)
            prev = cur
        return prev

    store(n_t - 1, lax.fori_loop(0, n_t // NBR_UNROLL, body, jnp.ones((2 * VT_ROWS, tq), F32)))


def _nbr_bias_kernel(rpb_ref, o_ref, *, rows):
    h = pl.program_id(0)
    n_dc = 2 * NA_KW - 1
    kcol = lax.broadcasted_iota(jnp.int32, (GRID_W, GRID_W), 0)
    col = lax.broadcasted_iota(jnp.int32, (GRID_W, GRID_W), 1)
    dc = kcol - col + NA_KW - 1
    cs = jnp.clip(col - NA_KW // 2, 0, GRID_W - NA_KW)
    col_ok = (kcol >= cs) & (kcol < cs + NA_KW)
    neg = jnp.full((GRID_W, GRID_W), NEG, F32)
    by_dr = []
    for dr in range(2 * NA_KH - 1):
        blk = neg
        for j in range(n_dc):
            blk = jnp.where(dc == j, rpb_ref[h, dr * n_dc + j] * LOG2_E, blk)
        by_dr.append(jnp.where(col_ok, blk, NEG))
    for cls, r0 in enumerate((0, NBR_ROWS, rows - NBR_ROWS)):
        kr0 = min(max(r0 - NA_KH // 2, 0), rows - NBR_KROWS)
        for kri in range(NBR_KROWS):
            krow = kr0 + kri
            blocks = []
            for ri in range(NBR_ROWS):
                rs = min(max(r0 + ri - NA_KH // 2, 0), rows - NA_KH)
                blocks.append(by_dr[krow - r0 - ri + NA_KH - 1] if rs <= krow < rs + NA_KH else neg)
            o_ref[cls, 0, kri * GRID_W:(kri + 1) * GRID_W, :] = jnp.concatenate(blocks, axis=-1)


def _nbr_bias(rpb, rows):
    n_heads = rpb.shape[0]
    tq, kk = NBR_ROWS * GRID_W, NBR_KROWS * GRID_W
    return pl.pallas_call(
        functools.partial(_nbr_bias_kernel, rows=rows),
        grid=(n_heads,),
        in_specs=[pl.BlockSpec(memory_space=pltpu.SMEM)],
        out_specs=pl.BlockSpec((3, 1, kk, tq), lambda h: (0, h, 0, 0)),
        out_shape=jax.ShapeDtypeStruct((3, n_heads, kk, tq), F32),
        compiler_params=_cparams(1),
        name="nbr_bias",
    )(rpb.astype(F32).reshape(n_heads, -1))


def _attn_nbr(q, k, vt, bias, *, n_batch, seq, ctx_len):
    r = n_batch * seq
    rows = seq // GRID_W
    tq = NBR_ROWS * GRID_W
    kk = NBR_KROWS * GRID_W
    tb = min(NBR_TILES, seq // tq) * tq
    nq = seq // tb
    cb = n_batch * seq // ctx_len
    return pl.pallas_call(
        functools.partial(_nbr_kernel, rows=rows),
        grid=(n_batch, C_HEADS // 2, nq),
        in_specs=[
            pl.BlockSpec((2, tb, HEAD_DIM), lambda b, hp, i: (hp, b * nq + i, 0)),
            pl.BlockSpec((2, seq, HEAD_DIM), lambda b, hp, i: (hp, b, 0)),
            pl.BlockSpec((2, VT_ROWS, seq), lambda b, hp, i: (hp, 0, b)),
            pl.BlockSpec((2, ctx_len, HEAD_DIM), lambda b, hp, i: (hp, cb + b, 0)),
            pl.BlockSpec((2, VT_ROWS, ctx_len), lambda b, hp, i: (hp, 0, cb + b)),
            pl.BlockSpec((3, 2, kk, tq), lambda b, hp, i: (0, hp, 0, 0)),
        ],
        out_specs=pl.BlockSpec((tb, 2 * HEAD_DIM), lambda b, hp, i: (b * nq + i, hp)),
        out_shape=jax.ShapeDtypeStruct((r, C_HEADS * HEAD_DIM), BF16),
        scratch_shapes=[pltpu.VMEM((2, 2, kk + ctx_len, tq), F32)],
        compiler_params=_cparams(3),
        name="attn_nbr",
    )(q, k, vt, k, vt, bias)


def _ctx_kernel(sink_ref, qw_ref, qg_ref, qn_ref, kw_ref, kg_ref, kn_ref, vw_ref, vg_ref, vn_ref,
                yw_o, yg_o, yn_o):
    def group(q_ref, k_ref, vt_ref, o_ref, n_heads, n_kv, sink=False):
        outs = []
        for h in range(n_heads):
            kv = h // (n_heads // n_kv)
            s = _dot_nt(k_ref[kv], q_ref[h])
            outs.append(_softmax_pv_t([(s, vt_ref[kv])],
                                      extra_logit=sink_ref[0, h] * LOG2_E if sink else None))
        o_ref[...] = jnp.concatenate(outs, axis=0).T.astype(o_ref.dtype)

    group(qw_ref, kw_ref, vw_ref, yw_o, A_HEADS, A_KV, sink=True)
    group(qg_ref, kg_ref, vg_ref, yg_o, B_HEADS, B_KV)
    group(qn_ref, kn_ref, vn_ref, yn_o, C_HEADS, C_HEADS)


def _attn_ctx(sink, qs, ks, vs, *, n_batch, seq, ctx_len):
    cb = n_batch * seq // ctx_len

    def hspec(a):
        if a.shape[1] == VT_ROWS:
            return pl.BlockSpec((a.shape[0], VT_ROWS, ctx_len), lambda b: (0, 0, cb + b))
        return pl.BlockSpec((a.shape[0], ctx_len, HEAD_DIM), lambda b: (0, cb + b, 0))

    widths = [q.shape[0] * HEAD_DIM for q in qs]
    return pl.pallas_call(
        _ctx_kernel,
        grid=(n_batch,),
        in_specs=[pl.BlockSpec(memory_space=pltpu.SMEM)] + [hspec(a) for a in (*qs, *ks, *vs)],
        out_specs=[pl.BlockSpec((ctx_len, w), lambda b: (b, 0)) for w in widths],
        out_shape=[jax.ShapeDtypeStruct((n_batch * ctx_len, w), BF16) for w in widths],
        compiler_params=_cparams(1),
        name="attn_ctx",
    )(sink.reshape(1, A_HEADS), *qs, *ks, *vs)


def _rope_tables(seq):
    t = jnp.arange(seq)
    row = (t // GRID_W).astype(F32)
    col = (t % GRID_W).astype(F32)
    n_freq = HEAD_DIM // 4
    inv = ROPE_THETA ** (-jnp.arange(n_freq, dtype=F32) / n_freq)
    ang = jnp.concatenate([row[:, None] * inv, col[:, None] * inv], axis=-1)
    cos, sin = jnp.cos(ang), jnp.sin(ang)
    cosf = jnp.concatenate([jnp.tile(cos, (1, 4)), jnp.ones((TM, LANES), F32)], axis=0)
    sinf = jnp.concatenate([jnp.tile(jnp.concatenate([-sin, sin], axis=-1), (1, 2)),
                            jnp.zeros((TM, LANES), F32)], axis=0)
    return cosf, sinf


def kernel(x, c, ctx, c_ctx, w_ada, b_ada, norm_ffn1, w_ffn1_gate, w_ffn1_up, w_ffn1_down,
           norm_mix, w_in, q_norm_glob, k_norm_glob, sink_win, rpb_nbr, w_out,
           norm_ffn2, w_ffn2_gate, w_ffn2_up, w_ffn2_down, norm_final):
    n_batch, seq, d = x.shape
    ctx_len = ctx.shape[1]
    depth = w_ada.shape[0]
    r_lat, r_ctx = n_batch * seq, n_batch * ctx_len
    assert seq % TM == 0 and r_ctx % TM == 0 and n_batch + 1 <= MOD_ROWS
    assert seq % (min(WIN_TILES, seq // TQ_WIN) * TQ_WIN) == 0
    assert seq % TQ_GLOB == 0 and seq % (2 * TK_GLOB) == 0
    assert seq % (NBR_ROWS * GRID_W) == 0 and r_lat % ctx_len == 0 and seq >= TQ_WIN + 2 * A_WINDOW
    n_lat, n_all, tpb = r_lat // TM, (r_lat + r_ctx) // TM, seq // TM
    tiles = dict(n_lat=n_lat, tpb=tpb, n_batch=n_batch)
    dims = dict(n_batch=n_batch, seq=seq, ctx_len=ctx_len)

    h_srcs = (x.reshape(r_lat, d), ctx.reshape(r_ctx, d))
    cc = jnp.zeros((MOD_ROWS, d), F32).at[:n_batch].set(c).at[n_batch].set(c_ctx)
    mod = _ada(cc, w_ada, b_ada).reshape(depth, MOD_ROWS, N_MOD, d)
    cosf, sinf = _rope_tables(seq)
    seg = jnp.asarray(np.kron(np.eye(2), np.full((HEAD_DIM, HEAD_DIM), 1.0 / HEAD_DIM)), BF16)

    ffn1 = [w.astype(BF16) for w in (w_ffn1_gate, w_ffn1_up, w_ffn1_down)]
    ffn2 = [w.astype(BF16) for w in (w_ffn2_gate, w_ffn2_up, w_ffn2_down)]
    w_in_b, w_out_b = w_in.astype(BF16), w_out.astype(BF16)

    for l in range(depth):
        last = l == depth - 1
        h = _ffn(h_srcs, mod, norm_ffn1, *ffn1, norm_final, layer=l, k0=0, n_tiles=n_all, final=False, **tiles)
        qw, qg, qn, kw, kg, kn, vw, vg, vn = _qkv(
            h, mod, norm_mix, w_in_b, q_norm_glob[l], k_norm_glob[l],
            cosf, sinf, seg, layer=l, n_tiles=n_all, **tiles)
        ys = [(_attn_window(qw, kw, vw, sink_win[l], **dims),),
              (_attn_global(qg, kg, vg, _logit_bound(q_norm_glob[l], k_norm_glob[l]), **dims),),
              (_attn_nbr(qn, kn, vn, _nbr_bias(rpb_nbr[l], seq // GRID_W), **dims),)]
        if not last:
            ycs = _attn_ctx(sink_win[l], (qw, qg, qn), (kw, kg, kn), (vw, vg, vn), **dims)
            ys = [(y, yc) for (y,), yc in zip(ys, ycs)]
        h = _ffn((h,), mod, norm_ffn2, *ffn2, norm_final, layer=l, k0=6, n_tiles=n_lat if last else n_all,
                 final=last, y_srcs=ys, w_out=w_out_b, **tiles)
        h_srcs = (h,)
    return h.reshape(n_batch, seq, d)
```

```python
import functools

import numpy as np
import jax
import jax.numpy as jnp
from jax import lax
from jax.experimental import pallas as pl
from jax.experimental.pallas import tpu as pltpu

D_MODEL = 1024
HEAD_DIM = 64
GRID_W = 64
A_HEADS, A_KV, A_WINDOW = 6, 2, 128
B_HEADS, B_KV = 4, 2
C_HEADS = 6
NA_KH, NA_KW = 8, 16
ROPE_THETA = 10000.0
EPS = 1e-6
N_MOD = 9
NEG = -1e30
MIX_WIDTH = D_MODEL
QK_SCALE = HEAD_DIM ** -0.5
LOG2_E = 1.4426950408889634
VT_ROWS = HEAD_DIM + 16
GLOB_SAFE_BOUND = 40.0

W_IN_HEADS = dict(qw=A_HEADS, qg=B_HEADS, qn=C_HEADS, kw=A_KV, vw=A_KV, kg=B_KV, vg=B_KV, kn=C_HEADS, vn=C_HEADS)

LANES = 128
MOD_ROWS = 8
TM = 512
TM_QKV = 2 * TM
TQ_WIN = 256
WIN_TILES = 16
WIN_UNROLL = 2
TQ_GLOB = 1024
TK_GLOB = 512
NBR_ROWS = 4
NBR_TILES = 16
NBR_UNROLL = 4
NBR_KROWS = NBR_ROWS + NA_KH
VMEM_LIMIT = 56 * 1024 * 1024

F32 = jnp.float32
BF16 = jnp.bfloat16


def _cparams(n_axes):
    return pltpu.CompilerParams(dimension_semantics=("arbitrary",) * n_axes,
                                vmem_limit_bytes=VMEM_LIMIT)


def _dot(a, b):
    return jnp.dot(a, b, preferred_element_type=F32)


def _dot_nt(a, b):
    return lax.dot_general(a, b, (((1,), (1,)), ((), ())), preferred_element_type=F32)


def _rms(x, g):
    return x * lax.rsqrt(jnp.mean(x * x, axis=-1, keepdims=True) + EPS) * g


def _split_bf16(x):
    hi = x.astype(BF16)
    return hi, (x - hi.astype(F32)).astype(BF16)


def _ada_kernel(cc_ref, w_ref, b_ref, o_ref):
    a = cc_ref[...]
    a = a * jax.nn.sigmoid(a)
    a_hi, a_lo = _split_bf16(a)
    w_hi, w_lo = _split_bf16(w_ref[0])
    o_ref[0] = _dot(a_hi, w_hi) + (_dot(a_lo, w_hi) + _dot(a_hi, w_lo)) + b_ref[0]


def _ada(cc, w_ada, b_ada):
    depth, d, n = w_ada.shape
    tn = 1024
    return pl.pallas_call(
        _ada_kernel,
        grid=(depth, n // tn),
        in_specs=[
            pl.BlockSpec((MOD_ROWS, d), lambda l, j: (0, 0)),
            pl.BlockSpec((1, d, tn), lambda l, j: (l, 0, j)),
            pl.BlockSpec((1, 1, tn), lambda l, j: (l, 0, j)),
        ],
        out_specs=pl.BlockSpec((1, MOD_ROWS, tn), lambda l, j: (l, 0, j)),
        out_shape=jax.ShapeDtypeStruct((depth, MOD_ROWS, n), F32),
        compiler_params=_cparams(2),
        name="ada",
    )(cc, w_ada, b_ada.reshape(depth, 1, n))


def _pick_tile(refs, n_lat):
    if len(refs) == 1:
        return refs[0][...]
    return jnp.where(pl.program_id(0) < n_lat, refs[0][...], refs[1][...])


def _ffn_kernel(*refs, k0, final, n_h, n_y, n_lat):
    refs = list(refs)
    h_refs = [refs.pop(0) for _ in range(n_h)]
    y_refs = [[refs.pop(0) for _ in range(n_y)] for _ in range(3 if n_y else 0)]
    mod_ref, g_ref, wg_ref, wu_ref, wd_ref, gf_ref = (refs.pop(0) for _ in range(6))
    wo_ref = refs.pop(0) if n_y else None
    (o_ref,) = refs
    h = _pick_tile(h_refs, n_lat)
    mod = mod_ref[0]
    if n_y:
        y = jnp.concatenate([_pick_tile(yr, n_lat) for yr in y_refs], axis=1)
        h = h + mod[5:6] * _dot(y, wo_ref[...])
    u = _rms(h, g_ref[...]) * (1.0 + mod[k0 + 1:k0 + 2]) + mod[k0:k0 + 1]
    ub = u.astype(BF16)
    gate = _dot(ub, wg_ref[...])
    up = _dot(ub, wu_ref[...])
    a = (gate * jax.nn.sigmoid(gate) * up).astype(BF16)
    y = _dot(a, wd_ref[...])
    out = h + 0.5 * mod[k0 + 2:k0 + 3] * y
    if final:
        out = _rms(out, gf_ref[...])
    o_ref[...] = out


def _row_maps(layer, n_lat, tpb, n_batch):
    def mod_map(i):
        return (layer, jnp.where(i < n_lat, i // tpb, n_batch), 0, 0)

    def rope_map(i):
        return (jnp.where(i < n_lat, i % tpb, tpb), 0)

    return mod_map, rope_map


def _resident(shape, layer=None):
    nd = len(shape)
    if layer is None:
        return pl.BlockSpec(shape, lambda i: (0,) * nd, pipeline_mode=pl.Buffered(1))
    return pl.BlockSpec((None, *shape[1:]), lambda i: (layer,) + (0,) * (nd - 1), pipeline_mode=pl.Buffered(1))


def _tile_specs(arrays, n_lat):
    if len(arrays) == 1:
        return [pl.BlockSpec((TM, arrays[0].shape[1]), lambda i: (i, 0))]
    return [pl.BlockSpec((TM, arrays[0].shape[1]), lambda i: (jnp.minimum(i, n_lat - 1), 0)),
            pl.BlockSpec((TM, arrays[1].shape[1]), lambda i: (jnp.maximum(i - n_lat, 0), 0))]


def _ffn(h_srcs, mod, gain, wg, wu, wd, gain_final, *, layer, k0, n_tiles, n_lat, tpb, n_batch, final,
         y_srcs=(), w_out=None):
    depth, d, f = wg.shape
    mod_map, _ = _row_maps(layer, n_lat, tpb, n_batch)
    n_y = len(y_srcs[0]) if y_srcs else 0
    streams = [h_srcs, *y_srcs]
    return pl.pallas_call(
        functools.partial(_ffn_kernel, k0=k0, final=final, n_h=len(h_srcs), n_y=n_y, n_lat=n_lat),
        grid=(n_tiles,),
        in_specs=[spec for s in streams for spec in _tile_specs(s, n_lat)] + [
            pl.BlockSpec((None, 1, N_MOD, d), mod_map),
            _resident((depth, 1, d), layer),
            _resident(wg.shape, layer),
            _resident(wu.shape, layer),
            _resident(wd.shape, layer),
            _resident((1, d)),
        ] + ([_resident(w_out.shape, layer)] if n_y else []),
        out_specs=pl.BlockSpec((TM, d), lambda i: (i, 0)),
        out_shape=jax.ShapeDtypeStruct((n_tiles * TM, d), F32),
        compiler_params=_cparams(1),
        name="ffn",
    )(*[a for s in streams for a in s], mod, gain.reshape(depth, 1, d), wg, wu, wd, gain_final.reshape(1, d),
      *([w_out] if n_y else []))


def _swap_halves(x):
    lane = lax.broadcasted_iota(jnp.int32, x.shape, 1)
    return jnp.where((lane % HEAD_DIM) < HEAD_DIM // 2,
                     pltpu.roll(x, LANES - HEAD_DIM // 2, 1),
                     pltpu.roll(x, HEAD_DIM // 2, 1))


def _head_mean_sq(x, seg):
    hi, lo = _split_bf16(x * x)
    return _dot(hi, seg) + _dot(lo, seg)


def _qkv_kernel(h_ref, mod_ref, g_ref, w_ref, qg_ref, kg_ref, cos_ref, sin_ref, seg_ref,
                qw_o, qgl_o, qn_o, kw_o, kgl_o, kn_o, vw_o, vgl_o, vn_o):
    mod = mod_ref[0]
    seg = seg_ref[...]
    q_scale = QK_SCALE * LOG2_E
    plan = {
        "qw": (qw_o, None, True, q_scale, False), "qg": (qgl_o, qg_ref, True, q_scale, False),
        "qn": (qn_o, None, False, q_scale, False), "kw": (kw_o, None, True, None, False),
        "vw": (vw_o, None, False, None, True), "kg": (kgl_o, kg_ref, True, None, False),
        "vg": (vgl_o, None, False, None, True), "kn": (kn_o, None, False, None, False),
        "vn": (vn_o, None, False, None, True),
    }

    for part in range(h_ref.shape[0] // TM):
        rows = slice(part * TM, (part + 1) * TM)
        u = _rms(h_ref[rows, :], g_ref[...]) * (1.0 + mod[4:5]) + mod[3:4]
        p = _dot(u.astype(BF16), w_ref[...])
        cosf = cos_ref[rows, :]
        sinf = sin_ref[rows, :]
        col = 0
        for name, n_heads in W_IN_HEADS.items():
            out_ref, gain_ref, rope, scale, transpose = plan[name]
            for c in range(n_heads // 2):
                xc = p[:, col:col + LANES]
                col += LANES
                if gain_ref is not None:
                    xc = xc * lax.rsqrt(_head_mean_sq(xc, seg) + EPS) * gain_ref[...]
                if rope:
                    xc = xc * cosf + _swap_halves(xc) * sinf
                if scale is not None:
                    xc = xc * scale
                if transpose:
                    xt = xc.T
                    ones = jnp.ones((VT_ROWS - HEAD_DIM, TM), BF16)
                    for j in range(2):
                        out_ref[2 * c + j, :HEAD_DIM, rows] = xt[j * HEAD_DIM:(j + 1) * HEAD_DIM].astype(BF16)
                        out_ref[2 * c + j, HEAD_DIM:, rows] = ones
                else:
                    out_ref[2 * c, rows, :] = xc[:, :HEAD_DIM].astype(BF16)
                    out_ref[2 * c + 1, rows, :] = xc[:, HEAD_DIM:].astype(BF16)


def _qkv(h, mod, gain, w_in, q_gain, k_gain, cosf, sinf, seg, *, layer, n_batch, seq):
    r, d = h.shape
    depth = w_in.shape[0]
    tm = TM_QKV
    mod_map, rope_map = _row_maps(layer, n_batch * seq // tm, seq // tm, n_batch)
    heads = (A_HEADS, B_HEADS, C_HEADS, A_KV, B_KV, C_HEADS, A_KV, B_KV, C_HEADS)
    transposed = tuple(j >= 6 for j in range(9))
    return pl.pallas_call(
        _qkv_kernel,
        grid=(r // tm,),
        in_specs=[
            pl.BlockSpec((tm, d), lambda i: (i, 0)),
            pl.BlockSpec((None, 1, N_MOD, d), mod_map),
            _resident((depth, 1, d), layer),
            _resident(w_in.shape, layer),
            _resident((1, LANES)),
            _resident((1, LANES)),
            pl.BlockSpec((tm, LANES), rope_map),
            pl.BlockSpec((tm, LANES), rope_map),
            _resident((LANES, LANES)),
        ],
        out_specs=[pl.BlockSpec((n, VT_ROWS, tm), lambda i: (0, 0, i)) if t else
                   pl.BlockSpec((n, tm, HEAD_DIM), lambda i: (0, i, 0)) for n, t in zip(heads, transposed)],
        out_shape=[jax.ShapeDtypeStruct((n, VT_ROWS, r) if t else (n, r, HEAD_DIM), BF16)
                   for n, t in zip(heads, transposed)],
        compiler_params=_cparams(1),
        name="qkv",
    )(h, mod, gain.reshape(depth, 1, d), w_in, jnp.tile(q_gain, 2).reshape(1, LANES),
      jnp.tile(k_gain, 2).reshape(1, LANES), cosf, sinf, seg)


def _softmax_pv_t(parts, extra_logit=None):
    m = functools.reduce(jnp.maximum, [jnp.max(s, axis=0, keepdims=True) for s, _ in parts])
    if extra_logit is not None:
        m = jnp.maximum(m, extra_logit)
    acc = None
    for s, vt in parts:
        pv = _dot(vt, jnp.exp2(s - m).astype(BF16))
        acc = pv if acc is None else acc + pv
    den = acc[HEAD_DIM:HEAD_DIM + 1]
    if extra_logit is not None:
        den = den + jnp.exp2(extra_logit - m)
    return acc[:HEAD_DIM] / den


def _win_kernel(sink_ref, q_ref, k_ref, vt_ref, kc_ref, vtc_ref, o_ref, s_scr, mask_scr, *, seq):
    tq = TQ_WIN
    n_t = q_ref.shape[1] // tq
    grp = A_HEADS // A_KV
    kb = tq + 2 * A_WINDOW
    tile0 = pl.program_id(1) * n_t
    last_tile = seq // tq - 1
    sinks = [jnp.concatenate([jnp.full((1, tq), sink_ref[0, kv * grp + g] * LOG2_E, F32)
                              for g in range(grp)], axis=1) for kv in range(A_KV)]

    @pl.when((pl.program_id(0) == 0) & (pl.program_id(1) == 0))
    def _():
        rel = (lax.broadcasted_iota(jnp.int32, (kb, tq), 0) - lax.broadcasted_iota(jnp.int32, (kb, tq), 1))
        for cls in range(3):
            ok = jnp.abs(rel - cls * A_WINDOW) <= A_WINDOW
            mask_scr[cls] = jnp.concatenate([jnp.where(ok, 0.0, NEG)] * grp, axis=1)

    def band(t):
        g = tile0 + t
        q0 = g * tq
        cls = jnp.where(g == 0, 0, jnp.where(g == last_tile, 2, 1))
        return cls, pl.multiple_of(jnp.clip(q0 - A_WINDOW, 0, seq - kb), A_WINDOW)

    def scores(slot, t):
        cls, ks = band(t)
        for kv in range(A_KV):
            q = q_ref[kv * grp:(kv + 1) * grp, pl.ds(pl.multiple_of(t * tq, tq), tq), :]
            q = q.reshape(grp * tq, HEAD_DIM)
            s_scr[slot, kv, :kb, :] = _dot_nt(k_ref[kv, pl.ds(ks, kb), :], q) + mask_scr[cls]
            s_scr[slot, kv, kb:, :] = _dot_nt(kc_ref[kv], q)

    def absorb(slot, t):
        _, ks = band(t)
        accs, sink_w = [], []
        for kv in range(A_KV):
            s = s_scr[slot, kv]
            m = jnp.maximum(jnp.max(s, axis=0, keepdims=True), sinks[kv])
            e = jnp.exp2(s - m).astype(BF16)
            accs.append(_dot(vt_ref[kv, :, pl.ds(ks, kb)], e[:kb]) + _dot(vtc_ref[kv], e[kb:]))
            sink_w.append(jnp.exp2(sinks[kv] - m))
        return jnp.concatenate(accs, axis=0), jnp.concatenate(sink_w, axis=0)

    def store(t, stage):
        acc, sink_w = stage
        outs = []
        for kv in range(A_KV):
            a = acc[kv * VT_ROWS:(kv + 1) * VT_ROWS]
            o = a[:HEAD_DIM] / (a[HEAD_DIM:HEAD_DIM + 1] + sink_w[kv:kv + 1])
            outs.extend(o[:, g * tq:(g + 1) * tq] for g in range(grp))
        o_ref[pl.ds(pl.multiple_of(t * tq, tq), tq), :] = jnp.concatenate(outs, axis=0).T.astype(o_ref.dtype)

    scores(0, 0)

    def body(j, prev):
        for u in range(WIN_UNROLL):
            t = WIN_UNROLL * j + u
            scores((u + 1) % 2, jnp.minimum(t + 1, n_t - 1))
            cur = absorb(u % 2, t)
            store(jnp.maximum(t - 1, 0), prev)
            prev = cur
        return prev

    init = (jnp.ones((A_KV * VT_ROWS, grp * tq), F32), jnp.zeros((A_KV, grp * tq), F32))
    store(n_t - 1, lax.fori_loop(0, n_t // WIN_UNROLL, body, init))


def _attn_window(q, k, vt, sink, *, n_batch, seq, ctx_len):
    r = n_batch * seq
    n_t = min(WIN_TILES, seq // TQ_WIN)
    tb = n_t * TQ_WIN
    nq = seq // tb
    cb = n_batch * seq // ctx_len
    return pl.pallas_call(
        functools.partial(_win_kernel, seq=seq),
        grid=(n_batch, nq),
        in_specs=[
            pl.BlockSpec(memory_space=pltpu.SMEM),
            pl.BlockSpec((A_HEADS, tb, HEAD_DIM), lambda b, i: (0, b * nq + i, 0)),
            pl.BlockSpec((A_KV, seq, HEAD_DIM), lambda b, i: (0, b, 0)),
            pl.BlockSpec((A_KV, VT_ROWS, seq), lambda b, i: (0, 0, b)),
            pl.BlockSpec((A_KV, ctx_len, HEAD_DIM), lambda b, i: (0, cb + b, 0)),
            pl.BlockSpec((A_KV, VT_ROWS, ctx_len), lambda b, i: (0, 0, cb + b)),
        ],
        out_specs=pl.BlockSpec((tb, A_HEADS * HEAD_DIM), lambda b, i: (b * nq + i, 0)),
        out_shape=jax.ShapeDtypeStruct((r, A_HEADS * HEAD_DIM), BF16),
        scratch_shapes=[pltpu.VMEM((2, A_KV, TQ_WIN + 2 * A_WINDOW + ctx_len, (A_HEADS // A_KV) * TQ_WIN), F32),
                        pltpu.VMEM((3, TQ_WIN + 2 * A_WINDOW, (A_HEADS // A_KV) * TQ_WIN), F32)],
        compiler_params=_cparams(2),
        name="attn_window",
    )(sink.reshape(1, A_HEADS), q, k, vt, k, vt)


def _glob_kernel(bound_ref, q_ref, k_ref, vt_ref, kc_ref, vtc_ref, o_ref, s_scr, *, seq):
    tq = q_ref.shape[1]
    grp = B_HEADS // B_KV
    n_chunks = seq // TK_GLOB
    bound = bound_ref[0, 0]

    def queries(kv):
        return q_ref[kv * grp:(kv + 1) * grp].reshape(grp * tq, HEAD_DIM)

    def scores(c, kv):
        off = pl.multiple_of(c * TK_GLOB, TK_GLOB)
        return _dot_nt(k_ref[kv, pl.ds(off, TK_GLOB), :], queries(kv))

    def vt_chunk(c, kv):
        return vt_ref[kv, :, pl.ds(pl.multiple_of(c * TK_GLOB, TK_GLOB), TK_GLOB)]

    def finish(accs):
        outs = []
        for acc in accs:
            o = acc[:HEAD_DIM] / acc[HEAD_DIM:HEAD_DIM + 1]
            outs.extend(o[:, g * tq:(g + 1) * tq] for g in range(grp))
        o_ref[...] = jnp.concatenate(outs, axis=0).T.astype(o_ref.dtype)

    @pl.when(bound <= GLOB_SAFE_BOUND)
    def _():
        def weights(s):
            return jnp.exp2(s - bound).astype(BF16)

        acc0 = tuple(_dot(vtc_ref[kv], weights(_dot_nt(kc_ref[kv], queries(kv)))) for kv in range(B_KV))

        def body(c, accs):
            return tuple(acc + _dot(vt_chunk(c, kv), weights(scores(c, kv))) for kv, acc in enumerate(accs))

        finish(lax.fori_loop(0, n_chunks, body, acc0, unroll=8))

    @pl.when(bound > GLOB_SAFE_BOUND)
    def _():
        def absorb(s, vt, m, acc):
            m_new = jnp.maximum(m, jnp.max(s, axis=0, keepdims=True))
            e = jnp.exp2(s - m_new).astype(BF16)
            return m_new, jnp.exp2(m - m_new) * acc + _dot(vt, e)

        accs = []
        for kv in range(B_KV):
            s = _dot_nt(kc_ref[kv], queries(kv))
            m0 = jnp.max(s, axis=0, keepdims=True)
            a0 = _dot(vtc_ref[kv], jnp.exp2(s - m0).astype(BF16))
            s_scr[0] = scores(0, kv)

            def body(j, carry, kv=kv):
                m, acc = carry
                s_scr[1] = scores(2 * j + 1, kv)
                m, acc = absorb(s_scr[0], vt_chunk(2 * j, kv), m, acc)
                s_scr[0] = scores(jnp.minimum(2 * j + 2, n_chunks - 1), kv)
                return absorb(s_scr[1], vt_chunk(2 * j + 1, kv), m, acc)

            accs.append(lax.fori_loop(0, n_chunks // 2, body, (m0, a0))[1])
        finish(accs)


def _logit_bound(q_gain, k_gain):
    bound = (HEAD_DIM * QK_SCALE * LOG2_E * 1.02) * jnp.max(jnp.abs(q_gain)) * jnp.max(jnp.abs(k_gain))
    return bound.astype(F32).reshape(1, 1)


def _attn_global(q, k, vt, bound, *, n_batch, seq, ctx_len):
    r = n_batch * seq
    tq = TQ_GLOB
    nq = seq // tq
    cb = n_batch * seq // ctx_len
    return pl.pallas_call(
        functools.partial(_glob_kernel, seq=seq),
        grid=(n_batch, nq),
        in_specs=[
            pl.BlockSpec(memory_space=pltpu.SMEM),
            pl.BlockSpec((B_HEADS, tq, HEAD_DIM), lambda b, i: (0, b * nq + i, 0)),
            pl.BlockSpec((B_KV, seq, HEAD_DIM), lambda b, i: (0, b, 0)),
            pl.BlockSpec((B_KV, VT_ROWS, seq), lambda b, i: (0, 0, b)),
            pl.BlockSpec((B_KV, ctx_len, HEAD_DIM), lambda b, i: (0, cb + b, 0)),
            pl.BlockSpec((B_KV, VT_ROWS, ctx_len), lambda b, i: (0, 0, cb + b)),
        ],
        out_specs=pl.BlockSpec((tq, B_HEADS * HEAD_DIM), lambda b, i: (b * nq + i, 0)),
        out_shape=jax.ShapeDtypeStruct((r, B_HEADS * HEAD_DIM), BF16),
        scratch_shapes=[pltpu.VMEM((2, TK_GLOB, (B_HEADS // B_KV) * tq), F32)],
        compiler_params=_cparams(2),
        name="attn_global",
    )(bound, q, k, vt, k, vt)


def _nbr_kernel(q_ref, k_ref, vt_ref, kc_ref, vtc_ref, bias_ref, o_ref, s_scr, *, rows):
    tq = NBR_ROWS * GRID_W
    kk = NBR_KROWS * GRID_W
    n_t = q_ref.shape[1] // tq
    tile0 = pl.program_id(2) * n_t
    last_tile = rows // NBR_ROWS - 1

    def band(t):
        g = tile0 + t
        cls = jnp.where(g == 0, 0, jnp.where(g == last_tile, 2, 1))
        kr0 = jnp.clip(g * NBR_ROWS - NA_KH // 2, 0, rows - NBR_KROWS)
        return cls, pl.multiple_of(kr0 * GRID_W, LANES)

    def scores(slot, t):
        cls, ks = band(t)
        for h in range(2):
            q = q_ref[h, pl.ds(pl.multiple_of(t * tq, tq), tq), :]
            s_scr[slot, h, :kk, :] = _dot_nt(k_ref[h, pl.ds(ks, kk), :], q) + bias_ref[cls, h]
            s_scr[slot, h, kk:, :] = _dot_nt(kc_ref[h], q)

    def absorb(slot, t):
        _, ks = band(t)
        accs = []
        for h in range(2):
            s = s_scr[slot, h]
            e = jnp.exp2(s - jnp.max(s, axis=0, keepdims=True)).astype(BF16)
            accs.append(_dot(vt_ref[h, :, pl.ds(ks, kk)], e[:kk]) + _dot(vtc_ref[h], e[kk:]))
        return jnp.concatenate(accs, axis=0)

    def store(t, acc):
        o = [acc[h * VT_ROWS:h * VT_ROWS + HEAD_DIM] / acc[h * VT_ROWS + HEAD_DIM:h * VT_ROWS + HEAD_DIM + 1]
             for h in range(2)]
        o_ref[pl.ds(pl.multiple_of(t * tq, tq), tq), :] = jnp.concatenate(o, axis=0).T.astype(o_ref.dtype)

    scores(0, 0)

    def body(j, prev):
        for u in range(NBR_UNROLL):
            t = NBR_UNROLL * j + u
            scores((u + 1) % 2, jnp.minimum(t + 1, n_t - 1))
            cur = absorb(u % 2, t)
            store(jnp.maximum(t - 1, 0), prev)
            prev = cur
        return prev

    store(n_t - 1, lax.fori_loop(0, n_t // NBR_UNROLL, body, jnp.ones((2 * VT_ROWS, tq), F32)))


def _nbr_bias_kernel(rpb_ref, o_ref, *, rows):
    h = pl.program_id(0)
    n_dc = 2 * NA_KW - 1
    kcol = lax.broadcasted_iota(jnp.int32, (GRID_W, GRID_W), 0)
    col = lax.broadcasted_iota(jnp.int32, (GRID_W, GRID_W), 1)
    dc = kcol - col + NA_KW - 1
    cs = jnp.clip(col - NA_KW // 2, 0, GRID_W - NA_KW)
    col_ok = (kcol >= cs) & (kcol < cs + NA_KW)
    neg = jnp.full((GRID_W, GRID_W), NEG, F32)
    by_dr = []
    for dr in range(2 * NA_KH - 1):
        blk = neg
        for j in range(n_dc):
            blk = jnp.where(dc == j, rpb_ref[h, dr * n_dc + j] * LOG2_E, blk)
        by_dr.append(jnp.where(col_ok, blk, NEG))
    for cls, r0 in enumerate((0, NBR_ROWS, rows - NBR_ROWS)):
        kr0 = min(max(r0 - NA_KH // 2, 0), rows - NBR_KROWS)
        for kri in range(NBR_KROWS):
            krow = kr0 + kri
            blocks = []
            for ri in range(NBR_ROWS):
                rs = min(max(r0 + ri - NA_KH // 2, 0), rows - NA_KH)
                blocks.append(by_dr[krow - r0 - ri + NA_KH - 1] if rs <= krow < rs + NA_KH else neg)
            o_ref[cls, 0, kri * GRID_W:(kri + 1) * GRID_W, :] = jnp.concatenate(blocks, axis=-1)


def _nbr_bias(rpb, rows):
    n_heads = rpb.shape[0]
    tq, kk = NBR_ROWS * GRID_W, NBR_KROWS * GRID_W
    return pl.pallas_call(
        functools.partial(_nbr_bias_kernel, rows=rows),
        grid=(n_heads,),
        in_specs=[pl.BlockSpec(memory_space=pltpu.SMEM)],
        out_specs=pl.BlockSpec((3, 1, kk, tq), lambda h: (0, h, 0, 0)),
        out_shape=jax.ShapeDtypeStruct((3, n_heads, kk, tq), F32),
        compiler_params=_cparams(1),
        name="nbr_bias",
    )(rpb.astype(F32).reshape(n_heads, -1))


def _attn_nbr(q, k, vt, bias, *, n_batch, seq, ctx_len):
    r = n_batch * seq
    rows = seq // GRID_W
    tq = NBR_ROWS * GRID_W
    kk = NBR_KROWS * GRID_W
    tb = min(NBR_TILES, seq // tq) * tq
    nq = seq // tb
    cb = n_batch * seq // ctx_len
    return pl.pallas_call(
        functools.partial(_nbr_kernel, rows=rows),
        grid=(n_batch, C_HEADS // 2, nq),
        in_specs=[
            pl.BlockSpec((2, tb, HEAD_DIM), lambda b, hp, i: (hp, b * nq + i, 0)),
            pl.BlockSpec((2, seq, HEAD_DIM), lambda b, hp, i: (hp, b, 0)),
            pl.BlockSpec((2, VT_ROWS, seq), lambda b, hp, i: (hp, 0, b)),
            pl.BlockSpec((2, ctx_len, HEAD_DIM), lambda b, hp, i: (hp, cb + b, 0)),
            pl.BlockSpec((2, VT_ROWS, ctx_len), lambda b, hp, i: (hp, 0, cb + b)),
            pl.BlockSpec((3, 2, kk, tq), lambda b, hp, i: (0, hp, 0, 0)),
        ],
        out_specs=pl.BlockSpec((tb, 2 * HEAD_DIM), lambda b, hp, i: (b * nq + i, hp)),
        out_shape=jax.ShapeDtypeStruct((r, C_HEADS * HEAD_DIM), BF16),
        scratch_shapes=[pltpu.VMEM((2, 2, kk + ctx_len, tq), F32)],
        compiler_params=_cparams(3),
        name="attn_nbr",
    )(q, k, vt, k, vt, bias)


def _ctx_kernel(sink_ref, qw_ref, qg_ref, qn_ref, kw_ref, kg_ref, kn_ref, vw_ref, vg_ref, vn_ref,
                yw_o, yg_o, yn_o):
    def group(q_ref, k_ref, vt_ref, o_ref, n_heads, n_kv, sink=False):
        outs = []
        for h in range(n_heads):
            kv = h // (n_heads // n_kv)
            s = _dot_nt(k_ref[kv], q_ref[h])
            outs.append(_softmax_pv_t([(s, vt_ref[kv])],
                                      extra_logit=sink_ref[0, h] * LOG2_E if sink else None))
        o_ref[...] = jnp.concatenate(outs, axis=0).T.astype(o_ref.dtype)

    group(qw_ref, kw_ref, vw_ref, yw_o, A_HEADS, A_KV, sink=True)
    group(qg_ref, kg_ref, vg_ref, yg_o, B_HEADS, B_KV)
    group(qn_ref, kn_ref, vn_ref, yn_o, C_HEADS, C_HEADS)


def _attn_ctx(sink, qs, ks, vs, *, n_batch, seq, ctx_len):
    cb = n_batch * seq // ctx_len

    def hspec(a):
        if a.shape[1] == VT_ROWS:
            return pl.BlockSpec((a.shape[0], VT_ROWS, ctx_len), lambda b: (0, 0, cb + b))
        return pl.BlockSpec((a.shape[0], ctx_len, HEAD_DIM), lambda b: (0, cb + b, 0))

    widths = [q.shape[0] * HEAD_DIM for q in qs]
    return pl.pallas_call(
        _ctx_kernel,
        grid=(n_batch,),
        in_specs=[pl.BlockSpec(memory_space=pltpu.SMEM)] + [hspec(a) for a in (*qs, *ks, *vs)],
        out_specs=[pl.BlockSpec((ctx_len, w), lambda b: (b, 0)) for w in widths],
        out_shape=[jax.ShapeDtypeStruct((n_batch * ctx_len, w), BF16) for w in widths],
        compiler_params=_cparams(1),
        name="attn_ctx",
    )(sink.reshape(1, A_HEADS), *qs, *ks, *vs)


def _rope_tables(seq):
    t = jnp.arange(seq)
    row = (t // GRID_W).astype(F32)
    col = (t % GRID_W).astype(F32)
    n_freq = HEAD_DIM // 4
    inv = ROPE_THETA ** (-jnp.arange(n_freq, dtype=F32) / n_freq)
    ang = jnp.concatenate([row[:, None] * inv, col[:, None] * inv], axis=-1)
    cos, sin = jnp.cos(ang), jnp.sin(ang)
    cosf = jnp.concatenate([jnp.tile(cos, (1, 4)), jnp.ones((TM_QKV, LANES), F32)], axis=0)
    sinf = jnp.concatenate([jnp.tile(jnp.concatenate([-sin, sin], axis=-1), (1, 2)),
                            jnp.zeros((TM_QKV, LANES), F32)], axis=0)
    return cosf, sinf


def kernel(x, c, ctx, c_ctx, w_ada, b_ada, norm_ffn1, w_ffn1_gate, w_ffn1_up, w_ffn1_down,
           norm_mix, w_in, q_norm_glob, k_norm_glob, sink_win, rpb_nbr, w_out,
           norm_ffn2, w_ffn2_gate, w_ffn2_up, w_ffn2_down, norm_final):
    n_batch, seq, d = x.shape
    ctx_len = ctx.shape[1]
    depth = w_ada.shape[0]
    r_lat, r_ctx = n_batch * seq, n_batch * ctx_len
    assert seq % TM_QKV == 0 and r_ctx % TM_QKV == 0 and n_batch + 1 <= MOD_ROWS
    assert seq % (min(WIN_TILES, seq // TQ_WIN) * TQ_WIN) == 0
    assert seq % TQ_GLOB == 0 and seq % (2 * TK_GLOB) == 0
    assert seq % (NBR_ROWS * GRID_W) == 0 and r_lat % ctx_len == 0 and seq >= TQ_WIN + 2 * A_WINDOW
    n_lat, n_all, tpb = r_lat // TM, (r_lat + r_ctx) // TM, seq // TM
    tiles = dict(n_lat=n_lat, tpb=tpb, n_batch=n_batch)
    dims = dict(n_batch=n_batch, seq=seq, ctx_len=ctx_len)

    h_srcs = (x.reshape(r_lat, d), ctx.reshape(r_ctx, d))
    cc = jnp.zeros((MOD_ROWS, d), F32).at[:n_batch].set(c).at[n_batch].set(c_ctx)
    mod = _ada(cc, w_ada, b_ada).reshape(depth, MOD_ROWS, N_MOD, d)
    cosf, sinf = _rope_tables(seq)
    seg = jnp.asarray(np.kron(np.eye(2), np.full((HEAD_DIM, HEAD_DIM), 1.0 / HEAD_DIM)), BF16)

    ffn1 = [w.astype(BF16) for w in (w_ffn1_gate, w_ffn1_up, w_ffn1_down)]
    ffn2 = [w.astype(BF16) for w in (w_ffn2_gate, w_ffn2_up, w_ffn2_down)]
    w_in_b, w_out_b = w_in.astype(BF16), w_out.astype(BF16)

    for l in range(depth):
        last = l == depth - 1
        h = _ffn(h_srcs, mod, norm_ffn1, *ffn1, norm_final, layer=l, k0=0, n_tiles=n_all, final=False, **tiles)
        qw, qg, qn, kw, kg, kn, vw, vg, vn = _qkv(
            h, mod, norm_mix, w_in_b, q_norm_glob[l], k_norm_glob[l],
            cosf, sinf, seg, layer=l, n_batch=n_batch, seq=seq)
        ys = [(_attn_window(qw, kw, vw, sink_win[l], **dims),),
              (_attn_global(qg, kg, vg, _logit_bound(q_norm_glob[l], k_norm_glob[l]), **dims),),
              (_attn_nbr(qn, kn, vn, _nbr_bias(rpb_nbr[l], seq // GRID_W), **dims),)]
        if not last:
            ycs = _attn_ctx(sink_win[l], (qw, qg, qn), (kw, kg, kn), (vw, vg, vn), **dims)
            ys = [(y, yc) for (y,), yc in zip(ys, ycs)]
        h = _ffn((h,), mod, norm_ffn2, *ffn2, norm_final, layer=l, k0=6, n_tiles=n_lat if last else n_all,
                 final=last, y_srcs=ys, w_out=w_out_b, **tiles)
        h_srcs = (h,)
    return h.reshape(n_batch, seq, d)
```

```python
import functools

import numpy as np
import jax
import jax.numpy as jnp
from jax import lax
from jax.experimental import pallas as pl
from jax.experimental.pallas import tpu as pltpu

HEAD_DIM = 64
GRID_W = 64
A_HEADS, A_KV, A_WINDOW = 6, 2, 128
B_HEADS, B_KV = 4, 2
C_HEADS = 6
NA_KH, NA_KW = 8, 16
ROPE_THETA = 10000.0
EPS = 1e-6
N_MOD = 9
MOD_FFN1, MOD_MIX, MOD_FFN2 = 0, 3, 6
NEG = -1e30
QK_SCALE = HEAD_DIM ** -0.5
LOG2_E = 1.4426950408889634
VT_ROWS = HEAD_DIM + 16
GLOB_SAFE_BOUND = 40.0

W_IN_HEADS = dict(qw=A_HEADS, qg=B_HEADS, qn=C_HEADS, kw=A_KV, vw=A_KV, kg=B_KV, vg=B_KV, kn=C_HEADS, vn=C_HEADS)

LANES = 128
MOD_ROWS = 8
ADA_TN = 1024
TM = 512
TM_QKV = 2 * TM
TQ_WIN = 256
WIN_TILES = 16
WIN_UNROLL = 2
TQ_GLOB = 1024
TK_GLOB = 512
NBR_ROWS = 4
NBR_TILES = 16
NBR_UNROLL = 4
NBR_KROWS = NBR_ROWS + NA_KH
VMEM_LIMIT = 56 * 1024 * 1024

F32 = jnp.float32
BF16 = jnp.bfloat16


def _cparams(n_axes):
    return pltpu.CompilerParams(dimension_semantics=("arbitrary",) * n_axes,
                                vmem_limit_bytes=VMEM_LIMIT)


def _dot(a, b):
    return jnp.dot(a, b, preferred_element_type=F32)


def _dot_nt(a, b):
    return lax.dot_general(a, b, (((1,), (1,)), ((), ())), preferred_element_type=F32)


def _rms(x, g):
    return x * lax.rsqrt(jnp.mean(x * x, axis=-1, keepdims=True) + EPS) * g


def _split_bf16(x):
    hi = x.astype(BF16)
    return hi, (x - hi.astype(F32)).astype(BF16)


def _ada_kernel(cc_ref, w_ref, b_ref, o_ref):
    a = cc_ref[...]
    a = a * jax.nn.sigmoid(a)
    a_hi, a_lo = _split_bf16(a)
    w_hi, w_lo = _split_bf16(w_ref[0])
    o_ref[0] = _dot(a_hi, w_hi) + (_dot(a_lo, w_hi) + _dot(a_hi, w_lo)) + b_ref[0]


def _ada(cc, w_ada, b_ada):
    depth, d, n = w_ada.shape
    tn = ADA_TN
    return pl.pallas_call(
        _ada_kernel,
        grid=(depth, n // tn),
        in_specs=[
            pl.BlockSpec((MOD_ROWS, d), lambda l, j: (0, 0)),
            pl.BlockSpec((1, d, tn), lambda l, j: (l, 0, j)),
            pl.BlockSpec((1, 1, tn), lambda l, j: (l, 0, j)),
        ],
        out_specs=pl.BlockSpec((1, MOD_ROWS, tn), lambda l, j: (l, 0, j)),
        out_shape=jax.ShapeDtypeStruct((depth, MOD_ROWS, n), F32),
        compiler_params=_cparams(2),
        name="ada",
    )(cc, w_ada, b_ada.reshape(depth, 1, n))


def _pick_tile(refs, n_lat):
    if len(refs) == 1:
        return refs[0][...]
    return jnp.where(pl.program_id(0) < n_lat, refs[0][...], refs[1][...])


def _ffn_kernel(*refs, k0, final, n_h, n_y, n_lat):
    refs = list(refs)
    h_refs = [refs.pop(0) for _ in range(n_h)]
    y_refs = [[refs.pop(0) for _ in range(n_y)] for _ in range(3 if n_y else 0)]
    mod_ref, g_ref, wg_ref, wu_ref, wd_ref, gf_ref = (refs.pop(0) for _ in range(6))
    wo_ref = refs.pop(0) if n_y else None
    (o_ref,) = refs
    h = _pick_tile(h_refs, n_lat)
    mod = mod_ref[0]
    if n_y:
        y = jnp.concatenate([_pick_tile(yr, n_lat) for yr in y_refs], axis=1)
        h = h + mod[MOD_MIX + 2:MOD_MIX + 3] * _dot(y, wo_ref[...])
    u = _rms(h, g_ref[...]) * (1.0 + mod[k0 + 1:k0 + 2]) + mod[k0:k0 + 1]
    ub = u.astype(BF16)
    gate = _dot(ub, wg_ref[...])
    up = _dot(ub, wu_ref[...])
    a = (gate * jax.nn.sigmoid(gate) * up).astype(BF16)
    y = _dot(a, wd_ref[...])
    out = h + 0.5 * mod[k0 + 2:k0 + 3] * y
    if final:
        out = _rms(out, gf_ref[...])
    o_ref[...] = out


def _row_maps(layer, n_lat, tpb, n_batch):
    def mod_map(i):
        return (layer, jnp.where(i < n_lat, i // tpb, n_batch), 0, 0)

    def rope_map(i):
        return (jnp.where(i < n_lat, i % tpb, tpb), 0)

    return mod_map, rope_map


def _resident(shape, layer=None):
    nd = len(shape)
    if layer is None:
        return pl.BlockSpec(shape, lambda i: (0,) * nd, pipeline_mode=pl.Buffered(1))
    return pl.BlockSpec((None, *shape[1:]), lambda i: (layer,) + (0,) * (nd - 1), pipeline_mode=pl.Buffered(1))


def _tile_specs(arrays, n_lat):
    if len(arrays) == 1:
        return [pl.BlockSpec((TM, arrays[0].shape[1]), lambda i: (i, 0))]
    return [pl.BlockSpec((TM, arrays[0].shape[1]), lambda i: (jnp.minimum(i, n_lat - 1), 0)),
            pl.BlockSpec((TM, arrays[1].shape[1]), lambda i: (jnp.maximum(i - n_lat, 0), 0))]


def _ffn(h_srcs, mod, gain, wg, wu, wd, gain_final, *, layer, k0, n_tiles, n_lat, tpb, n_batch, final,
         y_srcs=(), w_out=None):
    depth, d, f = wg.shape
    mod_map, _ = _row_maps(layer, n_lat, tpb, n_batch)
    n_y = len(y_srcs[0]) if y_srcs else 0
    streams = [h_srcs, *y_srcs]
    return pl.pallas_call(
        functools.partial(_ffn_kernel, k0=k0, final=final, n_h=len(h_srcs), n_y=n_y, n_lat=n_lat),
        grid=(n_tiles,),
        in_specs=[spec for s in streams for spec in _tile_specs(s, n_lat)] + [
            pl.BlockSpec((None, 1, N_MOD, d), mod_map),
            _resident((depth, 1, d), layer),
            _resident(wg.shape, layer),
            _resident(wu.shape, layer),
            _resident(wd.shape, layer),
            _resident((1, d)),
        ] + ([_resident(w_out.shape, layer)] if n_y else []),
        out_specs=pl.BlockSpec((TM, d), lambda i: (i, 0)),
        out_shape=jax.ShapeDtypeStruct((n_tiles * TM, d), F32),
        compiler_params=_cparams(1),
        name="ffn",
    )(*[a for s in streams for a in s], mod, gain.reshape(depth, 1, d), wg, wu, wd, gain_final.reshape(1, d),
      *([w_out] if n_y else []))


def _swap_halves(x):
    lane = lax.broadcasted_iota(jnp.int32, x.shape, 1)
    return jnp.where((lane % HEAD_DIM) < HEAD_DIM // 2,
                     pltpu.roll(x, LANES - HEAD_DIM // 2, 1),
                     pltpu.roll(x, HEAD_DIM // 2, 1))


def _head_mean_sq(x, seg):
    hi, lo = _split_bf16(x * x)
    return _dot(hi, seg) + _dot(lo, seg)


def _qkv_kernel(h_ref, mod_ref, g_ref, w_ref, qg_ref, kg_ref, cos_ref, sin_ref, seg_ref,
                qw_o, qgl_o, qn_o, kw_o, kgl_o, kn_o, vw_o, vgl_o, vn_o):
    mod = mod_ref[0]
    seg = seg_ref[...]
    q_scale = QK_SCALE * LOG2_E
    plan = {
        "qw": (qw_o, None, True, q_scale, False), "qg": (qgl_o, qg_ref, True, q_scale, False),
        "qn": (qn_o, None, False, q_scale, False), "kw": (kw_o, None, True, None, False),
        "vw": (vw_o, None, False, None, True), "kg": (kgl_o, kg_ref, True, None, False),
        "vg": (vgl_o, None, False, None, True), "kn": (kn_o, None, False, None, False),
        "vn": (vn_o, None, False, None, True),
    }

    for part in range(h_ref.shape[0] // TM):
        rows = slice(part * TM, (part + 1) * TM)
        u = (_rms(h_ref[rows, :], g_ref[...]) * (1.0 + mod[MOD_MIX + 1:MOD_MIX + 2])
             + mod[MOD_MIX:MOD_MIX + 1])
        p = _dot(u.astype(BF16), w_ref[...])
        cosf = cos_ref[rows, :]
        sinf = sin_ref[rows, :]
        col = 0
        for name, n_heads in W_IN_HEADS.items():
            out_ref, gain_ref, rope, scale, transpose = plan[name]
            for c in range(n_heads // 2):
                xc = p[:, col:col + LANES]
                col += LANES
                if gain_ref is not None:
                    xc = xc * lax.rsqrt(_head_mean_sq(xc, seg) + EPS) * gain_ref[...]
                if rope:
                    xc = xc * cosf + _swap_halves(xc) * sinf
                if scale is not None:
                    xc = xc * scale
                if transpose:
                    xt = xc.T
                    ones = jnp.ones((VT_ROWS - HEAD_DIM, TM), BF16)
                    for j in range(2):
                        out_ref[2 * c + j, :HEAD_DIM, rows] = xt[j * HEAD_DIM:(j + 1) * HEAD_DIM].astype(BF16)
                        out_ref[2 * c + j, HEAD_DIM:, rows] = ones
                else:
                    out_ref[2 * c, rows, :] = xc[:, :HEAD_DIM].astype(BF16)
                    out_ref[2 * c + 1, rows, :] = xc[:, HEAD_DIM:].astype(BF16)


def _qkv(h, mod, gain, w_in, q_gain, k_gain, cosf, sinf, seg, *, layer, n_batch, seq):
    r, d = h.shape
    depth = w_in.shape[0]
    tm = TM_QKV
    mod_map, rope_map = _row_maps(layer, n_batch * seq // tm, seq // tm, n_batch)
    heads = (A_HEADS, B_HEADS, C_HEADS, A_KV, B_KV, C_HEADS, A_KV, B_KV, C_HEADS)
    transposed = tuple(j >= 6 for j in range(9))
    return pl.pallas_call(
        _qkv_kernel,
        grid=(r // tm,),
        in_specs=[
            pl.BlockSpec((tm, d), lambda i: (i, 0)),
            pl.BlockSpec((None, 1, N_MOD, d), mod_map),
            _resident((depth, 1, d), layer),
            _resident(w_in.shape, layer),
            _resident((1, LANES)),
            _resident((1, LANES)),
            pl.BlockSpec((tm, LANES), rope_map),
            pl.BlockSpec((tm, LANES), rope_map),
            _resident((LANES, LANES)),
        ],
        out_specs=[pl.BlockSpec((n, VT_ROWS, tm), lambda i: (0, 0, i)) if t else
                   pl.BlockSpec((n, tm, HEAD_DIM), lambda i: (0, i, 0)) for n, t in zip(heads, transposed)],
        out_shape=[jax.ShapeDtypeStruct((n, VT_ROWS, r) if t else (n, r, HEAD_DIM), BF16)
                   for n, t in zip(heads, transposed)],
        compiler_params=_cparams(1),
        name="qkv",
    )(h, mod, gain.reshape(depth, 1, d), w_in, jnp.tile(q_gain, 2).reshape(1, LANES),
      jnp.tile(k_gain, 2).reshape(1, LANES), cosf, sinf, seg)


def _softmax_pv_t(parts, extra_logit=None):
    m = functools.reduce(jnp.maximum, [jnp.max(s, axis=0, keepdims=True) for s, _ in parts])
    if extra_logit is not None:
        m = jnp.maximum(m, extra_logit)
    acc = None
    for s, vt in parts:
        pv = _dot(vt, jnp.exp2(s - m).astype(BF16))
        acc = pv if acc is None else acc + pv
    den = acc[HEAD_DIM:HEAD_DIM + 1]
    if extra_logit is not None:
        den = den + jnp.exp2(extra_logit - m)
    return acc[:HEAD_DIM] / den


def _win_kernel(sink_ref, q_ref, k_ref, vt_ref, kc_ref, vtc_ref, o_ref, s_scr, mask_scr, *, seq):
    tq = TQ_WIN
    n_t = q_ref.shape[1] // tq
    grp = A_HEADS // A_KV
    kb = tq + 2 * A_WINDOW
    tile0 = pl.program_id(1) * n_t
    last_tile = seq // tq - 1
    sinks = [jnp.concatenate([jnp.full((1, tq), sink_ref[0, kv * grp + g] * LOG2_E, F32)
                              for g in range(grp)], axis=1) for kv in range(A_KV)]

    @pl.when((pl.program_id(0) == 0) & (pl.program_id(1) == 0))
    def _():
        rel = (lax.broadcasted_iota(jnp.int32, (kb, tq), 0) - lax.broadcasted_iota(jnp.int32, (kb, tq), 1))
        for cls in range(3):
            ok = jnp.abs(rel - cls * A_WINDOW) <= A_WINDOW
            mask_scr[cls] = jnp.concatenate([jnp.where(ok, 0.0, NEG)] * grp, axis=1)

    def band(t):
        g = tile0 + t
        q0 = g * tq
        cls = jnp.where(g == 0, 0, jnp.where(g == last_tile, 2, 1))
        return cls, pl.multiple_of(jnp.clip(q0 - A_WINDOW, 0, seq - kb), A_WINDOW)

    def scores(slot, t):
        cls, ks = band(t)
        for kv in range(A_KV):
            q = q_ref[kv * grp:(kv + 1) * grp, pl.ds(pl.multiple_of(t * tq, tq), tq), :]
            q = q.reshape(grp * tq, HEAD_DIM)
            s_scr[slot, kv, :kb, :] = _dot_nt(k_ref[kv, pl.ds(ks, kb), :], q) + mask_scr[cls]
            s_scr[slot, kv, kb:, :] = _dot_nt(kc_ref[kv], q)

    def absorb(slot, t):
        _, ks = band(t)
        accs, sink_w = [], []
        for kv in range(A_KV):
            s = s_scr[slot, kv]
            m = jnp.maximum(jnp.max(s, axis=0, keepdims=True), sinks[kv])
            e = jnp.exp2(s - m).astype(BF16)
            accs.append(_dot(vt_ref[kv, :, pl.ds(ks, kb)], e[:kb]) + _dot(vtc_ref[kv], e[kb:]))
            sink_w.append(jnp.exp2(sinks[kv] - m))
        return jnp.concatenate(accs, axis=0), jnp.concatenate(sink_w, axis=0)

    def store(t, stage):
        acc, sink_w = stage
        outs = []
        for kv in range(A_KV):
            a = acc[kv * VT_ROWS:(kv + 1) * VT_ROWS]
            o = a[:HEAD_DIM] / (a[HEAD_DIM:HEAD_DIM + 1] + sink_w[kv:kv + 1])
            outs.extend(o[:, g * tq:(g + 1) * tq] for g in range(grp))
        o_ref[pl.ds(pl.multiple_of(t * tq, tq), tq), :] = jnp.concatenate(outs, axis=0).T.astype(o_ref.dtype)

    scores(0, 0)

    def body(j, prev):
        for u in range(WIN_UNROLL):
            t = WIN_UNROLL * j + u
            scores((u + 1) % 2, jnp.minimum(t + 1, n_t - 1))
            cur = absorb(u % 2, t)
            store(jnp.maximum(t - 1, 0), prev)
            prev = cur
        return prev

    init = (jnp.ones((A_KV * VT_ROWS, grp * tq), F32), jnp.zeros((A_KV, grp * tq), F32))
    store(n_t - 1, lax.fori_loop(0, n_t // WIN_UNROLL, body, init))


def _attn_window(q, k, vt, sink, *, n_batch, seq, ctx_len):
    r = n_batch * seq
    n_t = min(WIN_TILES, seq // TQ_WIN)
    assert n_t % WIN_UNROLL == 0
    tb = n_t * TQ_WIN
    nq = seq // tb
    cb = n_batch * seq // ctx_len
    return pl.pallas_call(
        functools.partial(_win_kernel, seq=seq),
        grid=(n_batch, nq),
        in_specs=[
            pl.BlockSpec(memory_space=pltpu.SMEM),
            pl.BlockSpec((A_HEADS, tb, HEAD_DIM), lambda b, i: (0, b * nq + i, 0)),
            pl.BlockSpec((A_KV, seq, HEAD_DIM), lambda b, i: (0, b, 0)),
            pl.BlockSpec((A_KV, VT_ROWS, seq), lambda b, i: (0, 0, b)),
            pl.BlockSpec((A_KV, ctx_len, HEAD_DIM), lambda b, i: (0, cb + b, 0)),
            pl.BlockSpec((A_KV, VT_ROWS, ctx_len), lambda b, i: (0, 0, cb + b)),
        ],
        out_specs=pl.BlockSpec((tb, A_HEADS * HEAD_DIM), lambda b, i: (b * nq + i, 0)),
        out_shape=jax.ShapeDtypeStruct((r, A_HEADS * HEAD_DIM), BF16),
        scratch_shapes=[pltpu.VMEM((2, A_KV, TQ_WIN + 2 * A_WINDOW + ctx_len, (A_HEADS // A_KV) * TQ_WIN), F32),
                        pltpu.VMEM((3, TQ_WIN + 2 * A_WINDOW, (A_HEADS // A_KV) * TQ_WIN), F32)],
        compiler_params=_cparams(2),
        name="attn_window",
    )(sink.reshape(1, A_HEADS), q, k, vt, k, vt)


def _glob_kernel(bound_ref, q_ref, k_ref, vt_ref, kc_ref, vtc_ref, o_ref, s_scr, *, seq):
    tq = q_ref.shape[1]
    grp = B_HEADS // B_KV
    n_chunks = seq // TK_GLOB
    bound = bound_ref[0, 0]

    def queries(kv):
        return q_ref[kv * grp:(kv + 1) * grp].reshape(grp * tq, HEAD_DIM)

    def scores(c, kv):
        off = pl.multiple_of(c * TK_GLOB, TK_GLOB)
        return _dot_nt(k_ref[kv, pl.ds(off, TK_GLOB), :], queries(kv))

    def vt_chunk(c, kv):
        return vt_ref[kv, :, pl.ds(pl.multiple_of(c * TK_GLOB, TK_GLOB), TK_GLOB)]

    def finish(accs):
        outs = []
        for acc in accs:
            o = acc[:HEAD_DIM] / acc[HEAD_DIM:HEAD_DIM + 1]
            outs.extend(o[:, g * tq:(g + 1) * tq] for g in range(grp))
        o_ref[...] = jnp.concatenate(outs, axis=0).T.astype(o_ref.dtype)

    @pl.when(bound <= GLOB_SAFE_BOUND)
    def _():
        def weights(s):
            return jnp.exp2(s - bound).astype(BF16)

        acc0 = tuple(_dot(vtc_ref[kv], weights(_dot_nt(kc_ref[kv], queries(kv)))) for kv in range(B_KV))

        def body(c, accs):
            return tuple(acc + _dot(vt_chunk(c, kv), weights(scores(c, kv))) for kv, acc in enumerate(accs))

        finish(lax.fori_loop(0, n_chunks, body, acc0, unroll=8))

    @pl.when(bound > GLOB_SAFE_BOUND)
    def _():
        def absorb(s, vt, m, acc):
            m_new = jnp.maximum(m, jnp.max(s, axis=0, keepdims=True))
            e = jnp.exp2(s - m_new).astype(BF16)
            return m_new, jnp.exp2(m - m_new) * acc + _dot(vt, e)

        accs = []
        for kv in range(B_KV):
            s = _dot_nt(kc_ref[kv], queries(kv))
            m0 = jnp.max(s, axis=0, keepdims=True)
            a0 = _dot(vtc_ref[kv], jnp.exp2(s - m0).astype(BF16))
            s_scr[0] = scores(0, kv)

            def body(j, carry, kv=kv):
                m, acc = carry
                s_scr[1] = scores(2 * j + 1, kv)
                m, acc = absorb(s_scr[0], vt_chunk(2 * j, kv), m, acc)
                s_scr[0] = scores(jnp.minimum(2 * j + 2, n_chunks - 1), kv)
                return absorb(s_scr[1], vt_chunk(2 * j + 1, kv), m, acc)

            accs.append(lax.fori_loop(0, n_chunks // 2, body, (m0, a0))[1])
        finish(accs)


def _logit_bound(q_gain, k_gain):
    bound = (HEAD_DIM * QK_SCALE * LOG2_E * 1.02) * jnp.max(jnp.abs(q_gain)) * jnp.max(jnp.abs(k_gain))
    return bound.astype(F32).reshape(1, 1)


def _attn_global(q, k, vt, bound, *, n_batch, seq, ctx_len):
    r = n_batch * seq
    tq = TQ_GLOB
    nq = seq // tq
    cb = n_batch * seq // ctx_len
    return pl.pallas_call(
        functools.partial(_glob_kernel, seq=seq),
        grid=(n_batch, nq),
        in_specs=[
            pl.BlockSpec(memory_space=pltpu.SMEM),
            pl.BlockSpec((B_HEADS, tq, HEAD_DIM), lambda b, i: (0, b * nq + i, 0)),
            pl.BlockSpec((B_KV, seq, HEAD_DIM), lambda b, i: (0, b, 0)),
            pl.BlockSpec((B_KV, VT_ROWS, seq), lambda b, i: (0, 0, b)),
            pl.BlockSpec((B_KV, ctx_len, HEAD_DIM), lambda b, i: (0, cb + b, 0)),
            pl.BlockSpec((B_KV, VT_ROWS, ctx_len), lambda b, i: (0, 0, cb + b)),
        ],
        out_specs=pl.BlockSpec((tq, B_HEADS * HEAD_DIM), lambda b, i: (b * nq + i, 0)),
        out_shape=jax.ShapeDtypeStruct((r, B_HEADS * HEAD_DIM), BF16),
        scratch_shapes=[pltpu.VMEM((2, TK_GLOB, (B_HEADS // B_KV) * tq), F32)],
        compiler_params=_cparams(2),
        name="attn_global",
    )(bound, q, k, vt, k, vt)


def _nbr_kernel(q_ref, k_ref, vt_ref, kc_ref, vtc_ref, bias_ref, o_ref, s_scr, *, rows):
    tq = NBR_ROWS * GRID_W
    kk = NBR_KROWS * GRID_W
    n_t = q_ref.shape[1] // tq
    tile0 = pl.program_id(2) * n_t
    last_tile = rows // NBR_ROWS - 1

    def band(t):
        g = tile0 + t
        cls = jnp.where(g == 0, 0, jnp.where(g == last_tile, 2, 1))
        kr0 = jnp.clip(g * NBR_ROWS - NA_KH // 2, 0, rows - NBR_KROWS)
        return cls, pl.multiple_of(kr0 * GRID_W, LANES)

    def scores(slot, t):
        cls, ks = band(t)
        for h in range(2):
            q = q_ref[h, pl.ds(pl.multiple_of(t * tq, tq), tq), :]
            s_scr[slot, h, :kk, :] = _dot_nt(k_ref[h, pl.ds(ks, kk), :], q) + bias_ref[cls, h]
            s_scr[slot, h, kk:, :] = _dot_nt(kc_ref[h], q)

    def absorb(slot, t):
        _, ks = band(t)
        accs = []
        for h in range(2):
            s = s_scr[slot, h]
            e = jnp.exp2(s - jnp.max(s, axis=0, keepdims=True)).astype(BF16)
            accs.append(_dot(vt_ref[h, :, pl.ds(ks, kk)], e[:kk]) + _dot(vtc_ref[h], e[kk:]))
        return jnp.concatenate(accs, axis=0)

    def store(t, acc):
        o = [acc[h * VT_ROWS:h * VT_ROWS + HEAD_DIM] / acc[h * VT_ROWS + HEAD_DIM:h * VT_ROWS + HEAD_DIM + 1]
             for h in range(2)]
        o_ref[pl.ds(pl.multiple_of(t * tq, tq), tq), :] = jnp.concatenate(o, axis=0).T.astype(o_ref.dtype)

    scores(0, 0)

    def body(j, prev):
        for u in range(NBR_UNROLL):
            t = NBR_UNROLL * j + u
            scores((u + 1) % 2, jnp.minimum(t + 1, n_t - 1))
            cur = absorb(u % 2, t)
            store(jnp.maximum(t - 1, 0), prev)
            prev = cur
        return prev

    store(n_t - 1, lax.fori_loop(0, n_t // NBR_UNROLL, body, jnp.ones((2 * VT_ROWS, tq), F32)))


def _nbr_bias_kernel(rpb_ref, o_ref, *, rows):
    h = pl.program_id(0)
    n_dc = 2 * NA_KW - 1
    kcol = lax.broadcasted_iota(jnp.int32, (GRID_W, GRID_W), 0)
    col = lax.broadcasted_iota(jnp.int32, (GRID_W, GRID_W), 1)
    dc = kcol - col + NA_KW - 1
    cs = jnp.clip(col - NA_KW // 2, 0, GRID_W - NA_KW)
    col_ok = (kcol >= cs) & (kcol < cs + NA_KW)
    neg = jnp.full((GRID_W, GRID_W), NEG, F32)
    by_dr = []
    for dr in range(2 * NA_KH - 1):
        blk = neg
        for j in range(n_dc):
            blk = jnp.where(dc == j, rpb_ref[h, dr * n_dc + j] * LOG2_E, blk)
        by_dr.append(jnp.where(col_ok, blk, NEG))
    for cls, r0 in enumerate((0, NBR_ROWS, rows - NBR_ROWS)):
        kr0 = min(max(r0 - NA_KH // 2, 0), rows - NBR_KROWS)
        for kri in range(NBR_KROWS):
            krow = kr0 + kri
            blocks = []
            for ri in range(NBR_ROWS):
                rs = min(max(r0 + ri - NA_KH // 2, 0), rows - NA_KH)
                blocks.append(by_dr[krow - r0 - ri + NA_KH - 1] if rs <= krow < rs + NA_KH else neg)
            o_ref[cls, 0, kri * GRID_W:(kri + 1) * GRID_W, :] = jnp.concatenate(blocks, axis=-1)


def _nbr_bias(rpb, rows):
    n_heads = rpb.shape[0]
    tq, kk = NBR_ROWS * GRID_W, NBR_KROWS * GRID_W
    return pl.pallas_call(
        functools.partial(_nbr_bias_kernel, rows=rows),
        grid=(n_heads,),
        in_specs=[pl.BlockSpec(memory_space=pltpu.SMEM)],
        out_specs=pl.BlockSpec((3, 1, kk, tq), lambda h: (0, h, 0, 0)),
        out_shape=jax.ShapeDtypeStruct((3, n_heads, kk, tq), F32),
        compiler_params=_cparams(1),
        name="nbr_bias",
    )(rpb.astype(F32).reshape(n_heads, -1))


def _attn_nbr(q, k, vt, bias, *, n_batch, seq, ctx_len):
    r = n_batch * seq
    rows = seq // GRID_W
    tq = NBR_ROWS * GRID_W
    kk = NBR_KROWS * GRID_W
    n_t = min(NBR_TILES, seq // tq)
    assert n_t % NBR_UNROLL == 0 and seq % (n_t * tq) == 0 and rows >= NBR_KROWS
    tb = n_t * tq
    nq = seq // tb
    cb = n_batch * seq // ctx_len
    return pl.pallas_call(
        functools.partial(_nbr_kernel, rows=rows),
        grid=(n_batch, C_HEADS // 2, nq),
        in_specs=[
            pl.BlockSpec((2, tb, HEAD_DIM), lambda b, hp, i: (hp, b * nq + i, 0)),
            pl.BlockSpec((2, seq, HEAD_DIM), lambda b, hp, i: (hp, b, 0)),
            pl.BlockSpec((2, VT_ROWS, seq), lambda b, hp, i: (hp, 0, b)),
            pl.BlockSpec((2, ctx_len, HEAD_DIM), lambda b, hp, i: (hp, cb + b, 0)),
            pl.BlockSpec((2, VT_ROWS, ctx_len), lambda b, hp, i: (hp, 0, cb + b)),
            pl.BlockSpec((3, 2, kk, tq), lambda b, hp, i: (0, hp, 0, 0)),
        ],
        out_specs=pl.BlockSpec((tb, 2 * HEAD_DIM), lambda b, hp, i: (b * nq + i, hp)),
        out_shape=jax.ShapeDtypeStruct((r, C_HEADS * HEAD_DIM), BF16),
        scratch_shapes=[pltpu.VMEM((2, 2, kk + ctx_len, tq), F32)],
        compiler_params=_cparams(3),
        name="attn_nbr",
    )(q, k, vt, k, vt, bias)


def _ctx_kernel(sink_ref, qw_ref, qg_ref, qn_ref, kw_ref, kg_ref, kn_ref, vw_ref, vg_ref, vn_ref,
                yw_o, yg_o, yn_o):
    def group(q_ref, k_ref, vt_ref, o_ref, n_heads, n_kv, sink=False):
        outs = []
        for h in range(n_heads):
            kv = h // (n_heads // n_kv)
            s = _dot_nt(k_ref[kv], q_ref[h])
            outs.append(_softmax_pv_t([(s, vt_ref[kv])],
                                      extra_logit=sink_ref[0, h] * LOG2_E if sink else None))
        o_ref[...] = jnp.concatenate(outs, axis=0).T.astype(o_ref.dtype)

    group(qw_ref, kw_ref, vw_ref, yw_o, A_HEADS, A_KV, sink=True)
    group(qg_ref, kg_ref, vg_ref, yg_o, B_HEADS, B_KV)
    group(qn_ref, kn_ref, vn_ref, yn_o, C_HEADS, C_HEADS)


def _attn_ctx(sink, qs, ks, vs, *, n_batch, seq, ctx_len):
    cb = n_batch * seq // ctx_len

    def hspec(a):
        if a.shape[1] == VT_ROWS:
            return pl.BlockSpec((a.shape[0], VT_ROWS, ctx_len), lambda b: (0, 0, cb + b))
        return pl.BlockSpec((a.shape[0], ctx_len, HEAD_DIM), lambda b: (0, cb + b, 0))

    widths = [q.shape[0] * HEAD_DIM for q in qs]
    return pl.pallas_call(
        _ctx_kernel,
        grid=(n_batch,),
        in_specs=[pl.BlockSpec(memory_space=pltpu.SMEM)] + [hspec(a) for a in (*qs, *ks, *vs)],
        out_specs=[pl.BlockSpec((ctx_len, w), lambda b: (b, 0)) for w in widths],
        out_shape=[jax.ShapeDtypeStruct((n_batch * ctx_len, w), BF16) for w in widths],
        compiler_params=_cparams(1),
        name="attn_ctx",
    )(sink.reshape(1, A_HEADS), *qs, *ks, *vs)


def _rope_tables(seq):
    t = jnp.arange(seq)
    row = (t // GRID_W).astype(F32)
    col = (t % GRID_W).astype(F32)
    n_freq = HEAD_DIM // 4
    inv = ROPE_THETA ** (-jnp.arange(n_freq, dtype=F32) / n_freq)
    ang = jnp.concatenate([row[:, None] * inv, col[:, None] * inv], axis=-1)
    cos, sin = jnp.cos(ang), jnp.sin(ang)
    cosf = jnp.concatenate([jnp.tile(cos, (1, 4)), jnp.ones((TM_QKV, LANES), F32)], axis=0)
    sinf = jnp.concatenate([jnp.tile(jnp.concatenate([-sin, sin], axis=-1), (1, 2)),
                            jnp.zeros((TM_QKV, LANES), F32)], axis=0)
    return cosf, sinf


def kernel(x, c, ctx, c_ctx, w_ada, b_ada, norm_ffn1, w_ffn1_gate, w_ffn1_up, w_ffn1_down,
           norm_mix, w_in, q_norm_glob, k_norm_glob, sink_win, rpb_nbr, w_out,
           norm_ffn2, w_ffn2_gate, w_ffn2_up, w_ffn2_down, norm_final):
    n_batch, seq, d = x.shape
    ctx_len = ctx.shape[1]
    depth = w_ada.shape[0]
    r_lat, r_ctx = n_batch * seq, n_batch * ctx_len
    assert seq % TM_QKV == 0 and r_ctx % TM_QKV == 0 and n_batch + 1 <= MOD_ROWS
    assert seq % (min(WIN_TILES, seq // TQ_WIN) * TQ_WIN) == 0
    assert seq % TQ_GLOB == 0 and seq % (2 * TK_GLOB) == 0
    assert seq % (NBR_ROWS * GRID_W) == 0 and r_lat % ctx_len == 0 and seq >= TQ_WIN + 2 * A_WINDOW
    n_lat, n_all, tpb = r_lat // TM, (r_lat + r_ctx) // TM, seq // TM
    tiles = dict(n_lat=n_lat, tpb=tpb, n_batch=n_batch)
    dims = dict(n_batch=n_batch, seq=seq, ctx_len=ctx_len)

    h_srcs = (x.reshape(r_lat, d), ctx.reshape(r_ctx, d))
    cc = jnp.zeros((MOD_ROWS, d), F32).at[:n_batch].set(c).at[n_batch].set(c_ctx)
    mod = _ada(cc, w_ada, b_ada).reshape(depth, MOD_ROWS, N_MOD, d)
    cosf, sinf = _rope_tables(seq)
    seg = jnp.asarray(np.kron(np.eye(2), np.full((HEAD_DIM, HEAD_DIM), 1.0 / HEAD_DIM)), BF16)

    ffn1 = [w.astype(BF16) for w in (w_ffn1_gate, w_ffn1_up, w_ffn1_down)]
    ffn2 = [w.astype(BF16) for w in (w_ffn2_gate, w_ffn2_up, w_ffn2_down)]
    w_in_b, w_out_b = w_in.astype(BF16), w_out.astype(BF16)

    for l in range(depth):
        last = l == depth - 1
        h = _ffn(h_srcs, mod, norm_ffn1, *ffn1, norm_final, layer=l, k0=MOD_FFN1, n_tiles=n_all, final=False,
                 **tiles)
        qw, qg, qn, kw, kg, kn, vw, vg, vn = _qkv(
            h, mod, norm_mix, w_in_b, q_norm_glob[l], k_norm_glob[l],
            cosf, sinf, seg, layer=l, n_batch=n_batch, seq=seq)
        ys = [(_attn_window(qw, kw, vw, sink_win[l], **dims),),
              (_attn_global(qg, kg, vg, _logit_bound(q_norm_glob[l], k_norm_glob[l]), **dims),),
              (_attn_nbr(qn, kn, vn, _nbr_bias(rpb_nbr[l], seq // GRID_W), **dims),)]
        if not last:
            ycs = _attn_ctx(sink_win[l], (qw, qg, qn), (kw, kg, kn), (vw, vg, vn), **dims)
            ys = [(y, yc) for (y,), yc in zip(ys, ycs)]
        h = _ffn((h,), mod, norm_ffn2, *ffn2, norm_final, layer=l, k0=MOD_FFN2,
                 n_tiles=n_lat if last else n_all,
                 final=last, y_srcs=ys, w_out=w_out_b, **tiles)
        h_srcs = (h,)
    return h.reshape(n_batch, seq, d)
```

```python
import functools

import numpy as np
import jax
import jax.numpy as jnp
from jax import lax
from jax.experimental import pallas as pl
from jax.experimental.pallas import tpu as pltpu

HEAD_DIM = 64
GRID_W = 64
A_HEADS, A_KV, A_WINDOW = 6, 2, 128
B_HEADS, B_KV = 4, 2
C_HEADS = 6
NA_KH, NA_KW = 8, 16
ROPE_THETA = 10000.0
EPS = 1e-6
N_MOD = 9
MOD_FFN1, MOD_MIX, MOD_FFN2 = 0, 3, 6
NEG = -1e30
QK_SCALE = HEAD_DIM ** -0.5
LOG2_E = 1.4426950408889634
VT_ROWS = HEAD_DIM + 16
GLOB_SAFE_BOUND = 40.0

W_IN_HEADS = dict(qw=A_HEADS, qg=B_HEADS, qn=C_HEADS, kw=A_KV, vw=A_KV, kg=B_KV, vg=B_KV, kn=C_HEADS, vn=C_HEADS)

LANES = 128
MOD_ROWS = 8
ADA_TN = 1024
TM = 512
TM_QKV = 2 * TM
TQ_WIN = 256
WIN_TILES = 16
WIN_UNROLL = 2
TQ_GLOB = 1024
TK_GLOB = 512
NBR_ROWS = 4
NBR_TILES = 16
NBR_UNROLL = 4
NBR_KROWS = NBR_ROWS + NA_KH
VMEM_LIMIT = 56 * 1024 * 1024

F32 = jnp.float32
BF16 = jnp.bfloat16


def _cparams(n_axes):
    return pltpu.CompilerParams(dimension_semantics=("arbitrary",) * n_axes,
                                vmem_limit_bytes=VMEM_LIMIT)


def _dot(a, b):
    return jnp.dot(a, b, preferred_element_type=F32)


def _dot_nt(a, b):
    return lax.dot_general(a, b, (((1,), (1,)), ((), ())), preferred_element_type=F32)


def _rms(x, g):
    return x * lax.rsqrt(jnp.mean(x * x, axis=-1, keepdims=True) + EPS) * g


def _split_bf16(x):
    hi = x.astype(BF16)
    return hi, (x - hi.astype(F32)).astype(BF16)


def _ada_kernel(cc_ref, w_ref, b_ref, o_ref):
    a = cc_ref[...]
    a = a * jax.nn.sigmoid(a)
    a_hi, a_lo = _split_bf16(a)
    w_hi, w_lo = _split_bf16(w_ref[0])
    o_ref[0] = _dot(a_hi, w_hi) + (_dot(a_lo, w_hi) + _dot(a_hi, w_lo)) + b_ref[0]


def _ada(cc, w_ada, b_ada):
    depth, d, n = w_ada.shape
    tn = ADA_TN
    return pl.pallas_call(
        _ada_kernel,
        grid=(depth, n // tn),
        in_specs=[
            pl.BlockSpec((MOD_ROWS, d), lambda l, j: (0, 0)),
            pl.BlockSpec((1, d, tn), lambda l, j: (l, 0, j)),
            pl.BlockSpec((1, 1, tn), lambda l, j: (l, 0, j)),
        ],
        out_specs=pl.BlockSpec((1, MOD_ROWS, tn), lambda l, j: (l, 0, j)),
        out_shape=jax.ShapeDtypeStruct((depth, MOD_ROWS, n), F32),
        compiler_params=_cparams(2),
        name="ada",
    )(cc, w_ada, b_ada.reshape(depth, 1, n))


def _pick_tile(refs, n_lat):
    if len(refs) == 1:
        return refs[0][...]
    return jnp.where(pl.program_id(0) < n_lat, refs[0][...], refs[1][...])


def _ffn_kernel(*refs, k0, final, n_h, n_y, n_lat):
    refs = list(refs)
    h_refs = [refs.pop(0) for _ in range(n_h)]
    y_refs = [[refs.pop(0) for _ in range(n_y)] for _ in range(3 if n_y else 0)]
    mod_ref, g_ref, wg_ref, wu_ref, wd_ref, gf_ref = (refs.pop(0) for _ in range(6))
    wo_ref = refs.pop(0) if n_y else None
    (o_ref,) = refs
    h = _pick_tile(h_refs, n_lat)
    mod = mod_ref[0]
    if n_y:
        y = jnp.concatenate([_pick_tile(yr, n_lat) for yr in y_refs], axis=1)
        h = h + mod[MOD_MIX + 2:MOD_MIX + 3] * _dot(y, wo_ref[...])
    u = _rms(h, g_ref[...]) * (1.0 + mod[k0 + 1:k0 + 2]) + mod[k0:k0 + 1]
    ub = u.astype(BF16)
    gate = _dot(ub, wg_ref[...])
    up = _dot(ub, wu_ref[...])
    a = (gate * jax.nn.sigmoid(gate) * up).astype(BF16)
    y = _dot(a, wd_ref[...])
    out = h + 0.5 * mod[k0 + 2:k0 + 3] * y
    if final:
        out = _rms(out, gf_ref[...])
    o_ref[...] = out


def _row_maps(layer, n_lat, tpb, n_batch):
    def mod_map(i):
        return (layer, jnp.where(i < n_lat, i // tpb, n_batch), 0, 0)

    def rope_map(i):
        return (jnp.where(i < n_lat, i % tpb, tpb), 0)

    return mod_map, rope_map


def _resident(shape, layer=None):
    nd = len(shape)
    if layer is None:
        return pl.BlockSpec(shape, lambda i: (0,) * nd, pipeline_mode=pl.Buffered(1))
    return pl.BlockSpec((None, *shape[1:]), lambda i: (layer,) + (0,) * (nd - 1), pipeline_mode=pl.Buffered(1))


def _tile_specs(arrays, n_lat):
    if len(arrays) == 1:
        return [pl.BlockSpec((TM, arrays[0].shape[1]), lambda i: (i, 0))]
    return [pl.BlockSpec((TM, arrays[0].shape[1]), lambda i: (jnp.minimum(i, n_lat - 1), 0)),
            pl.BlockSpec((TM, arrays[1].shape[1]), lambda i: (jnp.maximum(i - n_lat, 0), 0))]


def _ffn(h_srcs, mod, gain, wg, wu, wd, gain_final, *, layer, k0, n_tiles, n_lat, tpb, n_batch, final,
         y_srcs=(), w_out=None):
    depth, d, f = wg.shape
    mod_map, _ = _row_maps(layer, n_lat, tpb, n_batch)
    n_y = len(y_srcs[0]) if y_srcs else 0
    streams = [h_srcs, *y_srcs]
    return pl.pallas_call(
        functools.partial(_ffn_kernel, k0=k0, final=final, n_h=len(h_srcs), n_y=n_y, n_lat=n_lat),
        grid=(n_tiles,),
        in_specs=[spec for s in streams for spec in _tile_specs(s, n_lat)] + [
            pl.BlockSpec((None, 1, N_MOD, d), mod_map),
            _resident((depth, 1, d), layer),
            _resident(wg.shape, layer),
            _resident(wu.shape, layer),
            _resident(wd.shape, layer),
            _resident((1, d)),
        ] + ([_resident(w_out.shape, layer)] if n_y else []),
        out_specs=pl.BlockSpec((TM, d), lambda i: (i, 0)),
        out_shape=jax.ShapeDtypeStruct((n_tiles * TM, d), F32),
        compiler_params=_cparams(1),
        name="ffn",
    )(*[a for s in streams for a in s], mod, gain.reshape(depth, 1, d), wg, wu, wd, gain_final.reshape(1, d),
      *([w_out] if n_y else []))


def _swap_halves(x):
    lane = lax.broadcasted_iota(jnp.int32, x.shape, 1)
    return jnp.where((lane % HEAD_DIM) < HEAD_DIM // 2,
                     pltpu.roll(x, LANES - HEAD_DIM // 2, 1),
                     pltpu.roll(x, HEAD_DIM // 2, 1))


def _head_mean_sq(x, seg):
    hi, lo = _split_bf16(x * x)
    return _dot(hi, seg) + _dot(lo, seg)


def _qkv_kernel(h_ref, mod_ref, g_ref, w_ref, qg_ref, kg_ref, cos_ref, sin_ref, seg_ref,
                qw_o, qgl_o, qn_o, kw_o, kgl_o, kn_o, vw_o, vgl_o, vn_o):
    mod = mod_ref[0]
    seg = seg_ref[...]
    q_scale = QK_SCALE * LOG2_E
    plan = {
        "qw": (qw_o, None, True, q_scale, False), "qg": (qgl_o, qg_ref, True, q_scale, False),
        "qn": (qn_o, None, False, q_scale, False), "kw": (kw_o, None, True, None, False),
        "vw": (vw_o, None, False, None, True), "kg": (kgl_o, kg_ref, True, None, False),
        "vg": (vgl_o, None, False, None, True), "kn": (kn_o, None, False, None, False),
        "vn": (vn_o, None, False, None, True),
    }

    for part in range(h_ref.shape[0] // TM):
        rows = slice(part * TM, (part + 1) * TM)
        u = (_rms(h_ref[rows, :], g_ref[...]) * (1.0 + mod[MOD_MIX + 1:MOD_MIX + 2])
             + mod[MOD_MIX:MOD_MIX + 1])
        p = _dot(u.astype(BF16), w_ref[...])
        cosf = cos_ref[rows, :]
        sinf = sin_ref[rows, :]
        col = 0
        for name, n_heads in W_IN_HEADS.items():
            out_ref, gain_ref, rope, scale, transpose = plan[name]
            for c in range(n_heads // 2):
                xc = p[:, col:col + LANES]
                col += LANES
                if gain_ref is not None:
                    xc = xc * lax.rsqrt(_head_mean_sq(xc, seg) + EPS) * gain_ref[...]
                if rope:
                    xc = xc * cosf + _swap_halves(xc) * sinf
                if scale is not None:
                    xc = xc * scale
                if transpose:
                    xt = xc.T
                    ones = jnp.ones((VT_ROWS - HEAD_DIM, TM), BF16)
                    for j in range(2):
                        out_ref[2 * c + j, :HEAD_DIM, rows] = xt[j * HEAD_DIM:(j + 1) * HEAD_DIM].astype(BF16)
                        out_ref[2 * c + j, HEAD_DIM:, rows] = ones
                else:
                    out_ref[2 * c, rows, :] = xc[:, :HEAD_DIM].astype(BF16)
                    out_ref[2 * c + 1, rows, :] = xc[:, HEAD_DIM:].astype(BF16)


def _qkv(h, mod, gain, w_in, q_gain, k_gain, cosf, sinf, seg, *, layer, n_batch, seq):
    r, d = h.shape
    depth = w_in.shape[0]
    tm = TM_QKV
    mod_map, rope_map = _row_maps(layer, n_batch * seq // tm, seq // tm, n_batch)
    heads = (A_HEADS, B_HEADS, C_HEADS, A_KV, B_KV, C_HEADS, A_KV, B_KV, C_HEADS)
    transposed = tuple(j >= 6 for j in range(9))
    return pl.pallas_call(
        _qkv_kernel,
        grid=(r // tm,),
        in_specs=[
            pl.BlockSpec((tm, d), lambda i: (i, 0)),
            pl.BlockSpec((None, 1, N_MOD, d), mod_map),
            _resident((depth, 1, d), layer),
            _resident(w_in.shape, layer),
            _resident((1, LANES)),
            _resident((1, LANES)),
            pl.BlockSpec((tm, LANES), rope_map),
            pl.BlockSpec((tm, LANES), rope_map),
            _resident((LANES, LANES)),
        ],
        out_specs=[pl.BlockSpec((n, VT_ROWS, tm), lambda i: (0, 0, i)) if t else
                   pl.BlockSpec((n, tm, HEAD_DIM), lambda i: (0, i, 0)) for n, t in zip(heads, transposed)],
        out_shape=[jax.ShapeDtypeStruct((n, VT_ROWS, r) if t else (n, r, HEAD_DIM), BF16)
                   for n, t in zip(heads, transposed)],
        compiler_params=_cparams(1),
        name="qkv",
    )(h, mod, gain.reshape(depth, 1, d), w_in, jnp.tile(q_gain, 2).reshape(1, LANES),
      jnp.tile(k_gain, 2).reshape(1, LANES), cosf, sinf, seg)


def _softmax_pv_t(parts, extra_logit=None):
    m = functools.reduce(jnp.maximum, [jnp.max(s, axis=0, keepdims=True) for s, _ in parts])
    if extra_logit is not None:
        m = jnp.maximum(m, extra_logit)
    acc = None
    for s, vt in parts:
        pv = _dot(vt, jnp.exp2(s - m).astype(BF16))
        acc = pv if acc is None else acc + pv
    den = acc[HEAD_DIM:HEAD_DIM + 1]
    if extra_logit is not None:
        den = den + jnp.exp2(extra_logit - m)
    return acc[:HEAD_DIM] / den


def _win_kernel(sink_ref, q_ref, k_ref, vt_ref, kc_ref, vtc_ref, o_ref, s_scr, mask_scr, *, seq):
    tq = TQ_WIN
    n_t = q_ref.shape[1] // tq
    grp = A_HEADS // A_KV
    kb = tq + 2 * A_WINDOW
    tile0 = pl.program_id(1) * n_t
    last_tile = seq // tq - 1
    sinks = [jnp.concatenate([jnp.full((1, tq), sink_ref[0, kv * grp + g] * LOG2_E, F32)
                              for g in range(grp)], axis=1) for kv in range(A_KV)]

    @pl.when((pl.program_id(0) == 0) & (pl.program_id(1) == 0))
    def _():
        rel = (lax.broadcasted_iota(jnp.int32, (kb, tq), 0) - lax.broadcasted_iota(jnp.int32, (kb, tq), 1))
        for cls in range(3):
            ok = jnp.abs(rel - cls * A_WINDOW) <= A_WINDOW
            mask_scr[cls] = jnp.concatenate([jnp.where(ok, 0.0, NEG)] * grp, axis=1)

    def band(t):
        g = tile0 + t
        q0 = g * tq
        cls = jnp.where(g == 0, 0, jnp.where(g == last_tile, 2, 1))
        return cls, pl.multiple_of(jnp.clip(q0 - A_WINDOW, 0, seq - kb), A_WINDOW)

    def scores(slot, t):
        cls, ks = band(t)
        for kv in range(A_KV):
            q = q_ref[kv * grp:(kv + 1) * grp, pl.ds(pl.multiple_of(t * tq, tq), tq), :]
            q = q.reshape(grp * tq, HEAD_DIM)
            s_scr[slot, kv, :kb, :] = _dot_nt(k_ref[kv, pl.ds(ks, kb), :], q) + mask_scr[cls]
            s_scr[slot, kv, kb:, :] = _dot_nt(kc_ref[kv], q)

    def absorb(slot, t):
        _, ks = band(t)
        accs, sink_w = [], []
        for kv in range(A_KV):
            s = s_scr[slot, kv]
            m = jnp.maximum(jnp.max(s, axis=0, keepdims=True), sinks[kv])
            e = jnp.exp2(s - m).astype(BF16)
            accs.append(_dot(vt_ref[kv, :, pl.ds(ks, kb)], e[:kb]) + _dot(vtc_ref[kv], e[kb:]))
            sink_w.append(jnp.exp2(sinks[kv] - m))
        return jnp.concatenate(accs, axis=0), jnp.concatenate(sink_w, axis=0)

    def store(t, stage):
        acc, sink_w = stage
        outs = []
        for kv in range(A_KV):
            a = acc[kv * VT_ROWS:(kv + 1) * VT_ROWS]
            o = a[:HEAD_DIM] / (a[HEAD_DIM:HEAD_DIM + 1] + sink_w[kv:kv + 1])
            outs.extend(o[:, g * tq:(g + 1) * tq] for g in range(grp))
        o_ref[pl.ds(pl.multiple_of(t * tq, tq), tq), :] = jnp.concatenate(outs, axis=0).T.astype(o_ref.dtype)

    scores(0, 0)

    def body(j, prev):
        for u in range(WIN_UNROLL):
            t = WIN_UNROLL * j + u
            scores((u + 1) % 2, jnp.minimum(t + 1, n_t - 1))
            cur = absorb(u % 2, t)
            store(jnp.maximum(t - 1, 0), prev)
            prev = cur
        return prev

    init = (jnp.ones((A_KV * VT_ROWS, grp * tq), F32), jnp.zeros((A_KV, grp * tq), F32))
    store(n_t - 1, lax.fori_loop(0, n_t // WIN_UNROLL, body, init))


def _attn_window(q, k, vt, sink, *, n_batch, seq, ctx_len):
    r = n_batch * seq
    n_t = min(WIN_TILES, seq // TQ_WIN)
    assert n_t % WIN_UNROLL == 0
    tb = n_t * TQ_WIN
    nq = seq // tb
    cb = n_batch * seq // ctx_len
    return pl.pallas_call(
        functools.partial(_win_kernel, seq=seq),
        grid=(n_batch, nq),
        in_specs=[
            pl.BlockSpec(memory_space=pltpu.SMEM),
            pl.BlockSpec((A_HEADS, tb, HEAD_DIM), lambda b, i: (0, b * nq + i, 0)),
            pl.BlockSpec((A_KV, seq, HEAD_DIM), lambda b, i: (0, b, 0)),
            pl.BlockSpec((A_KV, VT_ROWS, seq), lambda b, i: (0, 0, b)),
            pl.BlockSpec((A_KV, ctx_len, HEAD_DIM), lambda b, i: (0, cb + b, 0)),
            pl.BlockSpec((A_KV, VT_ROWS, ctx_len), lambda b, i: (0, 0, cb + b)),
        ],
        out_specs=pl.BlockSpec((tb, A_HEADS * HEAD_DIM), lambda b, i: (b * nq + i, 0)),
        out_shape=jax.ShapeDtypeStruct((r, A_HEADS * HEAD_DIM), BF16),
        scratch_shapes=[pltpu.VMEM((2, A_KV, TQ_WIN + 2 * A_WINDOW + ctx_len, (A_HEADS // A_KV) * TQ_WIN), F32),
                        pltpu.VMEM((3, TQ_WIN + 2 * A_WINDOW, (A_HEADS // A_KV) * TQ_WIN), F32)],
        compiler_params=_cparams(2),
        name="attn_window",
    )(sink.reshape(1, A_HEADS), q, k, vt, k, vt)


def _glob_kernel(bound_ref, q_ref, k_ref, vt_ref, kc_ref, vtc_ref, o_ref, s_scr, *, seq):
    tq = q_ref.shape[1]
    grp = B_HEADS // B_KV
    n_chunks = seq // TK_GLOB
    bound = bound_ref[0, 0]

    def queries(kv):
        return q_ref[kv * grp:(kv + 1) * grp].reshape(grp * tq, HEAD_DIM)

    def scores(c, kv):
        off = pl.multiple_of(c * TK_GLOB, TK_GLOB)
        return _dot_nt(k_ref[kv, pl.ds(off, TK_GLOB), :], queries(kv))

    def vt_chunk(c, kv):
        return vt_ref[kv, :, pl.ds(pl.multiple_of(c * TK_GLOB, TK_GLOB), TK_GLOB)]

    def finish(accs):
        outs = []
        for acc in accs:
            o = acc[:HEAD_DIM] / acc[HEAD_DIM:HEAD_DIM + 1]
            outs.extend(o[:, g * tq:(g + 1) * tq] for g in range(grp))
        o_ref[...] = jnp.concatenate(outs, axis=0).T.astype(o_ref.dtype)

    @pl.when(bound <= GLOB_SAFE_BOUND)
    def _():
        def weights(s):
            return jnp.exp2(s - bound).astype(BF16)

        acc0 = tuple(_dot(vtc_ref[h // grp], weights(_dot_nt(kc_ref[h // grp], q_ref[h]))) for h in range(B_HEADS))

        def body(c, accs):
            off = pl.multiple_of(c * TK_GLOB, TK_GLOB)
            return tuple(acc + _dot(vt_chunk(c, h // grp),
                                    weights(_dot_nt(k_ref[h // grp, pl.ds(off, TK_GLOB), :], q_ref[h])))
                         for h, acc in enumerate(accs))

        accs = lax.fori_loop(0, n_chunks, body, acc0, unroll=8)
        finish([jnp.concatenate(accs[kv * grp:(kv + 1) * grp], axis=1) for kv in range(B_KV)])

    @pl.when(bound > GLOB_SAFE_BOUND)
    def _():
        def absorb(s, vt, m, acc):
            m_new = jnp.maximum(m, jnp.max(s, axis=0, keepdims=True))
            e = jnp.exp2(s - m_new).astype(BF16)
            return m_new, jnp.exp2(m - m_new) * acc + _dot(vt, e)

        accs = []
        for kv in range(B_KV):
            s = _dot_nt(kc_ref[kv], queries(kv))
            m0 = jnp.max(s, axis=0, keepdims=True)
            a0 = _dot(vtc_ref[kv], jnp.exp2(s - m0).astype(BF16))
            s_scr[0] = scores(0, kv)

            def body(j, carry, kv=kv):
                m, acc = carry
                s_scr[1] = scores(2 * j + 1, kv)
                m, acc = absorb(s_scr[0], vt_chunk(2 * j, kv), m, acc)
                s_scr[0] = scores(jnp.minimum(2 * j + 2, n_chunks - 1), kv)
                return absorb(s_scr[1], vt_chunk(2 * j + 1, kv), m, acc)

            accs.append(lax.fori_loop(0, n_chunks // 2, body, (m0, a0))[1])
        finish(accs)


def _logit_bound(q_gain, k_gain):
    bound = (HEAD_DIM * QK_SCALE * LOG2_E * 1.02) * jnp.max(jnp.abs(q_gain)) * jnp.max(jnp.abs(k_gain))
    return bound.astype(F32).reshape(1, 1)


def _attn_global(q, k, vt, bound, *, n_batch, seq, ctx_len):
    r = n_batch * seq
    tq = TQ_GLOB
    nq = seq // tq
    cb = n_batch * seq // ctx_len
    return pl.pallas_call(
        functools.partial(_glob_kernel, seq=seq),
        grid=(n_batch, nq),
        in_specs=[
            pl.BlockSpec(memory_space=pltpu.SMEM),
            pl.BlockSpec((B_HEADS, tq, HEAD_DIM), lambda b, i: (0, b * nq + i, 0)),
            pl.BlockSpec((B_KV, seq, HEAD_DIM), lambda b, i: (0, b, 0)),
            pl.BlockSpec((B_KV, VT_ROWS, seq), lambda b, i: (0, 0, b)),
            pl.BlockSpec((B_KV, ctx_len, HEAD_DIM), lambda b, i: (0, cb + b, 0)),
            pl.BlockSpec((B_KV, VT_ROWS, ctx_len), lambda b, i: (0, 0, cb + b)),
        ],
        out_specs=pl.BlockSpec((tq, B_HEADS * HEAD_DIM), lambda b, i: (b * nq + i, 0)),
        out_shape=jax.ShapeDtypeStruct((r, B_HEADS * HEAD_DIM), BF16),
        scratch_shapes=[pltpu.VMEM((2, TK_GLOB, (B_HEADS // B_KV) * tq), F32)],
        compiler_params=_cparams(2),
        name="attn_global",
    )(bound, q, k, vt, k, vt)


def _nbr_kernel(q_ref, k_ref, vt_ref, kc_ref, vtc_ref, bias_ref, o_ref, s_scr, *, rows):
    tq = NBR_ROWS * GRID_W
    kk = NBR_KROWS * GRID_W
    n_t = q_ref.shape[1] // tq
    tile0 = pl.program_id(2) * n_t
    last_tile = rows // NBR_ROWS - 1

    def band(t):
        g = tile0 + t
        cls = jnp.where(g == 0, 0, jnp.where(g == last_tile, 2, 1))
        kr0 = jnp.clip(g * NBR_ROWS - NA_KH // 2, 0, rows - NBR_KROWS)
        return cls, pl.multiple_of(kr0 * GRID_W, LANES)

    def scores(slot, t):
        cls, ks = band(t)
        for h in range(2):
            q = q_ref[h, pl.ds(pl.multiple_of(t * tq, tq), tq), :]
            s_scr[slot, h, :kk, :] = _dot_nt(k_ref[h, pl.ds(ks, kk), :], q) + bias_ref[cls, h]
            s_scr[slot, h, kk:, :] = _dot_nt(kc_ref[h], q)

    def absorb(slot, t):
        _, ks = band(t)
        accs = []
        for h in range(2):
            s = s_scr[slot, h]
            e = jnp.exp2(s - jnp.max(s, axis=0, keepdims=True)).astype(BF16)
            accs.append(_dot(vt_ref[h, :, pl.ds(ks, kk)], e[:kk]) + _dot(vtc_ref[h], e[kk:]))
        return jnp.concatenate(accs, axis=0)

    def store(t, acc):
        o = [acc[h * VT_ROWS:h * VT_ROWS + HEAD_DIM] / acc[h * VT_ROWS + HEAD_DIM:h * VT_ROWS + HEAD_DIM + 1]
             for h in range(2)]
        o_ref[pl.ds(pl.multiple_of(t * tq, tq), tq), :] = jnp.concatenate(o, axis=0).T.astype(o_ref.dtype)

    scores(0, 0)

    def body(j, prev):
        for u in range(NBR_UNROLL):
            t = NBR_UNROLL * j + u
            scores((u + 1) % 2, jnp.minimum(t + 1, n_t - 1))
            cur = absorb(u % 2, t)
            store(jnp.maximum(t - 1, 0), prev)
            prev = cur
        return prev

    store(n_t - 1, lax.fori_loop(0, n_t // NBR_UNROLL, body, jnp.ones((2 * VT_ROWS, tq), F32)))


def _nbr_bias_kernel(rpb_ref, o_ref, *, rows):
    h = pl.program_id(0)
    n_dc = 2 * NA_KW - 1
    kcol = lax.broadcasted_iota(jnp.int32, (GRID_W, GRID_W), 0)
    col = lax.broadcasted_iota(jnp.int32, (GRID_W, GRID_W), 1)
    dc = kcol - col + NA_KW - 1
    cs = jnp.clip(col - NA_KW // 2, 0, GRID_W - NA_KW)
    col_ok = (kcol >= cs) & (kcol < cs + NA_KW)
    neg = jnp.full((GRID_W, GRID_W), NEG, F32)
    by_dr = []
    for dr in range(2 * NA_KH - 1):
        blk = neg
        for j in range(n_dc):
            blk = jnp.where(dc == j, rpb_ref[h, dr * n_dc + j] * LOG2_E, blk)
        by_dr.append(jnp.where(col_ok, blk, NEG))
    for cls, r0 in enumerate((0, NBR_ROWS, rows - NBR_ROWS)):
        kr0 = min(max(r0 - NA_KH // 2, 0), rows - NBR_KROWS)
        for kri in range(NBR_KROWS):
            krow = kr0 + kri
            blocks = []
            for ri in range(NBR_ROWS):
                rs = min(max(r0 + ri - NA_KH // 2, 0), rows - NA_KH)
                blocks.append(by_dr[krow - r0 - ri + NA_KH - 1] if rs <= krow < rs + NA_KH else neg)
            o_ref[cls, 0, kri * GRID_W:(kri + 1) * GRID_W, :] = jnp.concatenate(blocks, axis=-1)


def _nbr_bias(rpb, rows):
    n_heads = rpb.shape[0]
    tq, kk = NBR_ROWS * GRID_W, NBR_KROWS * GRID_W
    return pl.pallas_call(
        functools.partial(_nbr_bias_kernel, rows=rows),
        grid=(n_heads,),
        in_specs=[pl.BlockSpec(memory_space=pltpu.SMEM)],
        out_specs=pl.BlockSpec((3, 1, kk, tq), lambda h: (0, h, 0, 0)),
        out_shape=jax.ShapeDtypeStruct((3, n_heads, kk, tq), F32),
        compiler_params=_cparams(1),
        name="nbr_bias",
    )(rpb.astype(F32).reshape(n_heads, -1))


def _attn_nbr(q, k, vt, bias, *, n_batch, seq, ctx_len):
    r = n_batch * seq
    rows = seq // GRID_W
    tq = NBR_ROWS * GRID_W
    kk = NBR_KROWS * GRID_W
    n_t = min(NBR_TILES, seq // tq)
    assert n_t % NBR_UNROLL == 0 and seq % (n_t * tq) == 0 and rows >= NBR_KROWS
    tb = n_t * tq
    nq = seq // tb
    cb = n_batch * seq // ctx_len
    return pl.pallas_call(
        functools.partial(_nbr_kernel, rows=rows),
        grid=(n_batch, C_HEADS // 2, nq),
        in_specs=[
            pl.BlockSpec((2, tb, HEAD_DIM), lambda b, hp, i: (hp, b * nq + i, 0)),
            pl.BlockSpec((2, seq, HEAD_DIM), lambda b, hp, i: (hp, b, 0)),
            pl.BlockSpec((2, VT_ROWS, seq), lambda b, hp, i: (hp, 0, b)),
            pl.BlockSpec((2, ctx_len, HEAD_DIM), lambda b, hp, i: (hp, cb + b, 0)),
            pl.BlockSpec((2, VT_ROWS, ctx_len), lambda b, hp, i: (hp, 0, cb + b)),
            pl.BlockSpec((3, 2, kk, tq), lambda b, hp, i: (0, hp, 0, 0)),
        ],
        out_specs=pl.BlockSpec((tb, 2 * HEAD_DIM), lambda b, hp, i: (b * nq + i, hp)),
        out_shape=jax.ShapeDtypeStruct((r, C_HEADS * HEAD_DIM), BF16),
        scratch_shapes=[pltpu.VMEM((2, 2, kk + ctx_len, tq), F32)],
        compiler_params=_cparams(3),
        name="attn_nbr",
    )(q, k, vt, k, vt, bias)


def _ctx_kernel(sink_ref, qw_ref, qg_ref, qn_ref, kw_ref, kg_ref, kn_ref, vw_ref, vg_ref, vn_ref,
                yw_o, yg_o, yn_o):
    def group(q_ref, k_ref, vt_ref, o_ref, n_heads, n_kv, sink=False):
        outs = []
        for h in range(n_heads):
            kv = h // (n_heads // n_kv)
            s = _dot_nt(k_ref[kv], q_ref[h])
            outs.append(_softmax_pv_t([(s, vt_ref[kv])],
                                      extra_logit=sink_ref[0, h] * LOG2_E if sink else None))
        o_ref[...] = jnp.concatenate(outs, axis=0).T.astype(o_ref.dtype)

    group(qw_ref, kw_ref, vw_ref, yw_o, A_HEADS, A_KV, sink=True)
    group(qg_ref, kg_ref, vg_ref, yg_o, B_HEADS, B_KV)
    group(qn_ref, kn_ref, vn_ref, yn_o, C_HEADS, C_HEADS)


def _attn_ctx(sink, qs, ks, vs, *, n_batch, seq, ctx_len):
    cb = n_batch * seq // ctx_len

    def hspec(a):
        if a.shape[1] == VT_ROWS:
            return pl.BlockSpec((a.shape[0], VT_ROWS, ctx_len), lambda b: (0, 0, cb + b))
        return pl.BlockSpec((a.shape[0], ctx_len, HEAD_DIM), lambda b: (0, cb + b, 0))

    widths = [q.shape[0] * HEAD_DIM for q in qs]
    return pl.pallas_call(
        _ctx_kernel,
        grid=(n_batch,),
        in_specs=[pl.BlockSpec(memory_space=pltpu.SMEM)] + [hspec(a) for a in (*qs, *ks, *vs)],
        out_specs=[pl.BlockSpec((ctx_len, w), lambda b: (b, 0)) for w in widths],
        out_shape=[jax.ShapeDtypeStruct((n_batch * ctx_len, w), BF16) for w in widths],
        compiler_params=_cparams(1),
        name="attn_ctx",
    )(sink.reshape(1, A_HEADS), *qs, *ks, *vs)


def _rope_tables(seq):
    t = jnp.arange(seq)
    row = (t // GRID_W).astype(F32)
    col = (t % GRID_W).astype(F32)
    n_freq = HEAD_DIM // 4
    inv = ROPE_THETA ** (-jnp.arange(n_freq, dtype=F32) / n_freq)
    ang = jnp.concatenate([row[:, None] * inv, col[:, None] * inv], axis=-1)
    cos, sin = jnp.cos(ang), jnp.sin(ang)
    cosf = jnp.concatenate([jnp.tile(cos, (1, 4)), jnp.ones((TM_QKV, LANES), F32)], axis=0)
    sinf = jnp.concatenate([jnp.tile(jnp.concatenate([-sin, sin], axis=-1), (1, 2)),
                            jnp.zeros((TM_QKV, LANES), F32)], axis=0)
    return cosf, sinf


def kernel(x, c, ctx, c_ctx, w_ada, b_ada, norm_ffn1, w_ffn1_gate, w_ffn1_up, w_ffn1_down,
           norm_mix, w_in, q_norm_glob, k_norm_glob, sink_win, rpb_nbr, w_out,
           norm_ffn2, w_ffn2_gate, w_ffn2_up, w_ffn2_down, norm_final):
    n_batch, seq, d = x.shape
    ctx_len = ctx.shape[1]
    depth = w_ada.shape[0]
    r_lat, r_ctx = n_batch * seq, n_batch * ctx_len
    assert seq % TM_QKV == 0 and r_ctx % TM_QKV == 0 and n_batch + 1 <= MOD_ROWS
    assert seq % (min(WIN_TILES, seq // TQ_WIN) * TQ_WIN) == 0
    assert seq % TQ_GLOB == 0 and seq % (2 * TK_GLOB) == 0
    assert seq % (NBR_ROWS * GRID_W) == 0 and r_lat % ctx_len == 0 and seq >= TQ_WIN + 2 * A_WINDOW
    n_lat, n_all, tpb = r_lat // TM, (r_lat + r_ctx) // TM, seq // TM
    tiles = dict(n_lat=n_lat, tpb=tpb, n_batch=n_batch)
    dims = dict(n_batch=n_batch, seq=seq, ctx_len=ctx_len)

    h_srcs = (x.reshape(r_lat, d), ctx.reshape(r_ctx, d))
    cc = jnp.zeros((MOD_ROWS, d), F32).at[:n_batch].set(c).at[n_batch].set(c_ctx)
    mod = _ada(cc, w_ada, b_ada).reshape(depth, MOD_ROWS, N_MOD, d)
    cosf, sinf = _rope_tables(seq)
    seg = jnp.asarray(np.kron(np.eye(2), np.full((HEAD_DIM, HEAD_DIM), 1.0 / HEAD_DIM)), BF16)

    ffn1 = [w.astype(BF16) for w in (w_ffn1_gate, w_ffn1_up, w_ffn1_down)]
    ffn2 = [w.astype(BF16) for w in (w_ffn2_gate, w_ffn2_up, w_ffn2_down)]
    w_in_b, w_out_b = w_in.astype(BF16), w_out.astype(BF16)

    for l in range(depth):
        last = l == depth - 1
        h = _ffn(h_srcs, mod, norm_ffn1, *ffn1, norm_final, layer=l, k0=MOD_FFN1, n_tiles=n_all, final=False,
                 **tiles)
        qw, qg, qn, kw, kg, kn, vw, vg, vn = _qkv(
            h, mod, norm_mix, w_in_b, q_norm_glob[l], k_norm_glob[l],
            cosf, sinf, seg, layer=l, n_batch=n_batch, seq=seq)
        ys = [(_attn_window(qw, kw, vw, sink_win[l], **dims),),
              (_attn_global(qg, kg, vg, _logit_bound(q_norm_glob[l], k_norm_glob[l]), **dims),),
              (_attn_nbr(qn, kn, vn, _nbr_bias(rpb_nbr[l], seq // GRID_W), **dims),)]
        if not last:
            ycs = _attn_ctx(sink_win[l], (qw, qg, qn), (kw, kg, kn), (vw, vg, vn), **dims)
            ys = [(y, yc) for (y,), yc in zip(ys, ycs)]
        h = _ffn((h,), mod, norm_ffn2, *ffn2, norm_final, layer=l, k0=MOD_FFN2,
                 n_tiles=n_lat if last else n_all,
                 final=last, y_srcs=ys, w_out=w_out_b, **tiles)
        h_srcs = (h,)
    return h.reshape(n_batch, seq, d)
```

```python
import functools

import numpy as np
import jax
import jax.numpy as jnp
from jax import lax
from jax.experimental import pallas as pl
from jax.experimental.pallas import tpu as pltpu

HEAD_DIM = 64
GRID_W = 64
A_HEADS, A_KV, A_WINDOW = 6, 2, 128
B_HEADS, B_KV = 4, 2
C_HEADS = 6
NA_KH, NA_KW = 8, 16
ROPE_THETA = 10000.0
EPS = 1e-6
N_MOD = 9
MOD_FFN1, MOD_MIX, MOD_FFN2 = 0, 3, 6
NEG = -1e30
QK_SCALE = HEAD_DIM ** -0.5
LOG2_E = 1.4426950408889634
VT_ROWS = HEAD_DIM + 16
GLOB_SAFE_BOUND = 40.0

W_IN_HEADS = dict(qw=A_HEADS, qg=B_HEADS, qn=C_HEADS, kw=A_KV, vw=A_KV, kg=B_KV, vg=B_KV, kn=C_HEADS, vn=C_HEADS)

LANES = 128
MOD_ROWS = 8
ADA_TN = 1024
TM = 512
TM_QKV = 2 * TM
TQ_WIN = 256
WIN_TILES = 16
WIN_UNROLL = 2
TQ_GLOB = 1024
TK_GLOB = 512
NBR_ROWS = 4
NBR_TILES = 16
NBR_UNROLL = 8
NBR_KROWS = NBR_ROWS + NA_KH
VMEM_LIMIT = 56 * 1024 * 1024

F32 = jnp.float32
BF16 = jnp.bfloat16


def _cparams(n_axes):
    return pltpu.CompilerParams(dimension_semantics=("arbitrary",) * n_axes,
                                vmem_limit_bytes=VMEM_LIMIT)


def _dot(a, b):
    return jnp.dot(a, b, preferred_element_type=F32)


def _dot_nt(a, b):
    return lax.dot_general(a, b, (((1,), (1,)), ((), ())), preferred_element_type=F32)


def _rms(x, g):
    return x * lax.rsqrt(jnp.mean(x * x, axis=-1, keepdims=True) + EPS) * g


def _split_bf16(x):
    hi = x.astype(BF16)
    return hi, (x - hi.astype(F32)).astype(BF16)


def _ada_kernel(cc_ref, w_ref, b_ref, o_ref):
    a = cc_ref[...]
    a = a * jax.nn.sigmoid(a)
    a_hi, a_lo = _split_bf16(a)
    w_hi, w_lo = _split_bf16(w_ref[0])
    o_ref[0] = _dot(a_hi, w_hi) + (_dot(a_lo, w_hi) + _dot(a_hi, w_lo)) + b_ref[0]


def _ada(cc, w_ada, b_ada):
    depth, d, n = w_ada.shape
    tn = ADA_TN
    return pl.pallas_call(
        _ada_kernel,
        grid=(depth, n // tn),
        in_specs=[
            pl.BlockSpec((MOD_ROWS, d), lambda l, j: (0, 0)),
            pl.BlockSpec((1, d, tn), lambda l, j: (l, 0, j)),
            pl.BlockSpec((1, 1, tn), lambda l, j: (l, 0, j)),
        ],
        out_specs=pl.BlockSpec((1, MOD_ROWS, tn), lambda l, j: (l, 0, j)),
        out_shape=jax.ShapeDtypeStruct((depth, MOD_ROWS, n), F32),
        compiler_params=_cparams(2),
        name="ada",
    )(cc, w_ada, b_ada.reshape(depth, 1, n))


def _pick_tile(refs, n_lat, rows):
    if len(refs) == 1:
        return refs[0][rows, :]
    return jnp.where(pl.program_id(0) < n_lat, refs[0][rows, :], refs[1][rows, :])


def _ffn_kernel(*refs, k0, final, n_h, n_y, n_lat):
    refs = list(refs)
    h_refs = [refs.pop(0) for _ in range(n_h)]
    y_refs = [[refs.pop(0) for _ in range(n_y)] for _ in range(3 if n_y else 0)]
    mod_ref, g_ref, wg_ref, wu_ref, wd_ref, gf_ref = (refs.pop(0) for _ in range(6))
    wo_ref = refs.pop(0) if n_y else None
    (o_ref,) = refs
    mod = mod_ref[0]
    n_parts = 1 if n_y else 2
    for part in range(n_parts):
        rows = slice(part * (TM // n_parts), (part + 1) * (TM // n_parts))
        h = _pick_tile(h_refs, n_lat, rows)
        if n_y:
            y = jnp.concatenate([_pick_tile(yr, n_lat, rows) for yr in y_refs], axis=1)
            h = h + mod[MOD_MIX + 2:MOD_MIX + 3] * _dot(y, wo_ref[...])
        u = _rms(h, g_ref[...]) * (1.0 + mod[k0 + 1:k0 + 2]) + mod[k0:k0 + 1]
        ub = u.astype(BF16)
        gate = _dot(ub, wg_ref[...])
        up = _dot(ub, wu_ref[...])
        a = (gate * jax.nn.sigmoid(gate) * up).astype(BF16)
        y = _dot(a, wd_ref[...])
        out = h + 0.5 * mod[k0 + 2:k0 + 3] * y
        if final:
            out = _rms(out, gf_ref[...])
        o_ref[rows, :] = out


def _row_maps(layer, n_lat, tpb, n_batch):
    def mod_map(i):
        return (layer, jnp.where(i < n_lat, i // tpb, n_batch), 0, 0)

    def rope_map(i):
        return (jnp.where(i < n_lat, i % tpb, tpb), 0)

    return mod_map, rope_map


def _resident(shape, layer=None):
    nd = len(shape)
    if layer is None:
        return pl.BlockSpec(shape, lambda i: (0,) * nd, pipeline_mode=pl.Buffered(1))
    return pl.BlockSpec((None, *shape[1:]), lambda i: (layer,) + (0,) * (nd - 1), pipeline_mode=pl.Buffered(1))


def _tile_specs(arrays, n_lat):
    if len(arrays) == 1:
        return [pl.BlockSpec((TM, arrays[0].shape[1]), lambda i: (i, 0))]
    return [pl.BlockSpec((TM, arrays[0].shape[1]), lambda i: (jnp.minimum(i, n_lat - 1), 0)),
            pl.BlockSpec((TM, arrays[1].shape[1]), lambda i: (jnp.maximum(i - n_lat, 0), 0))]


def _ffn(h_srcs, mod, gain, wg, wu, wd, gain_final, *, layer, k0, n_tiles, n_lat, tpb, n_batch, final,
         y_srcs=(), w_out=None):
    depth, d, f = wg.shape
    mod_map, _ = _row_maps(layer, n_lat, tpb, n_batch)
    n_y = len(y_srcs[0]) if y_srcs else 0
    streams = [h_srcs, *y_srcs]
    return pl.pallas_call(
        functools.partial(_ffn_kernel, k0=k0, final=final, n_h=len(h_srcs), n_y=n_y, n_lat=n_lat),
        grid=(n_tiles,),
        in_specs=[spec for s in streams for spec in _tile_specs(s, n_lat)] + [
            pl.BlockSpec((None, 1, N_MOD, d), mod_map),
            _resident((depth, 1, d), layer),
            _resident(wg.shape, layer),
            _resident(wu.shape, layer),
            _resident(wd.shape, layer),
            _resident((1, d)),
        ] + ([_resident(w_out.shape, layer)] if n_y else []),
        out_specs=pl.BlockSpec((TM, d), lambda i: (i, 0)),
        out_shape=jax.ShapeDtypeStruct((n_tiles * TM, d), F32),
        compiler_params=_cparams(1),
        name="ffn",
    )(*[a for s in streams for a in s], mod, gain.reshape(depth, 1, d), wg, wu, wd, gain_final.reshape(1, d),
      *([w_out] if n_y else []))


def _swap_halves(x):
    lane = lax.broadcasted_iota(jnp.int32, x.shape, 1)
    return jnp.where((lane % HEAD_DIM) < HEAD_DIM // 2,
                     pltpu.roll(x, LANES - HEAD_DIM // 2, 1),
                     pltpu.roll(x, HEAD_DIM // 2, 1))


def _head_mean_sq(x, seg):
    hi, lo = _split_bf16(x * x)
    return _dot(hi, seg) + _dot(lo, seg)


def _qkv_kernel(h_ref, mod_ref, g_ref, w_ref, qg_ref, kg_ref, cos_ref, sin_ref, seg_ref,
                qw_o, qgl_o, qn_o, kw_o, kgl_o, kn_o, vw_o, vgl_o, vn_o):
    mod = mod_ref[0]
    seg = seg_ref[...]
    q_scale = QK_SCALE * LOG2_E
    plan = {
        "qw": (qw_o, None, True, q_scale, False), "qg": (qgl_o, qg_ref, True, q_scale, False),
        "qn": (qn_o, None, False, q_scale, False), "kw": (kw_o, None, True, None, False),
        "vw": (vw_o, None, False, None, True), "kg": (kgl_o, kg_ref, True, None, False),
        "vg": (vgl_o, None, False, None, True), "kn": (kn_o, None, False, None, False),
        "vn": (vn_o, None, False, None, True),
    }

    for part in range(h_ref.shape[0] // TM):
        rows = slice(part * TM, (part + 1) * TM)
        u = (_rms(h_ref[rows, :], g_ref[...]) * (1.0 + mod[MOD_MIX + 1:MOD_MIX + 2])
             + mod[MOD_MIX:MOD_MIX + 1])
        p = _dot(u.astype(BF16), w_ref[...])
        cosf = cos_ref[rows, :]
        sinf = sin_ref[rows, :]
        col = 0
        for name, n_heads in W_IN_HEADS.items():
            out_ref, gain_ref, rope, scale, transpose = plan[name]
            for c in range(n_heads // 2):
                xc = p[:, col:col + LANES]
                col += LANES
                if gain_ref is not None:
                    xc = xc * lax.rsqrt(_head_mean_sq(xc, seg) + EPS) * gain_ref[...]
                if rope:
                    xc = xc * cosf + _swap_halves(xc) * sinf
                if scale is not None:
                    xc = xc * scale
                if transpose:
                    xt = xc.T
                    ones = jnp.ones((VT_ROWS - HEAD_DIM, TM), BF16)
                    for j in range(2):
                        out_ref[2 * c + j, :HEAD_DIM, rows] = xt[j * HEAD_DIM:(j + 1) * HEAD_DIM].astype(BF16)
                        out_ref[2 * c + j, HEAD_DIM:, rows] = ones
                else:
                    out_ref[2 * c, rows, :] = xc[:, :HEAD_DIM].astype(BF16)
                    out_ref[2 * c + 1, rows, :] = xc[:, HEAD_DIM:].astype(BF16)


def _qkv(h, mod, gain, w_in, q_gain, k_gain, cosf, sinf, seg, *, layer, n_batch, seq):
    r, d = h.shape
    depth = w_in.shape[0]
    tm = TM_QKV
    mod_map, rope_map = _row_maps(layer, n_batch * seq // tm, seq // tm, n_batch)
    heads = (A_HEADS, B_HEADS, C_HEADS, A_KV, B_KV, C_HEADS, A_KV, B_KV, C_HEADS)
    transposed = tuple(j >= 6 for j in range(9))
    return pl.pallas_call(
        _qkv_kernel,
        grid=(r // tm,),
        in_specs=[
            pl.BlockSpec((tm, d), lambda i: (i, 0)),
            pl.BlockSpec((None, 1, N_MOD, d), mod_map),
            _resident((depth, 1, d), layer),
            _resident(w_in.shape, layer),
            _resident((1, LANES)),
            _resident((1, LANES)),
            pl.BlockSpec((tm, LANES), rope_map),
            pl.BlockSpec((tm, LANES), rope_map),
            _resident((LANES, LANES)),
        ],
        out_specs=[pl.BlockSpec((n, VT_ROWS, tm), lambda i: (0, 0, i)) if t else
                   pl.BlockSpec((n, tm, HEAD_DIM), lambda i: (0, i, 0)) for n, t in zip(heads, transposed)],
        out_shape=[jax.ShapeDtypeStruct((n, VT_ROWS, r) if t else (n, r, HEAD_DIM), BF16)
                   for n, t in zip(heads, transposed)],
        compiler_params=_cparams(1),
        name="qkv",
    )(h, mod, gain.reshape(depth, 1, d), w_in, jnp.tile(q_gain, 2).reshape(1, LANES),
      jnp.tile(k_gain, 2).reshape(1, LANES), cosf, sinf, seg)


def _softmax_pv_t(parts, extra_logit=None):
    m = functools.reduce(jnp.maximum, [jnp.max(s, axis=0, keepdims=True) for s, _ in parts])
    if extra_logit is not None:
        m = jnp.maximum(m, extra_logit)
    acc = None
    for s, vt in parts:
        pv = _dot(vt, jnp.exp2(s - m).astype(BF16))
        acc = pv if acc is None else acc + pv
    den = acc[HEAD_DIM:HEAD_DIM + 1]
    if extra_logit is not None:
        den = den + jnp.exp2(extra_logit - m)
    return acc[:HEAD_DIM] / den


def _win_kernel(sink_ref, q_ref, k_ref, vt_ref, kc_ref, vtc_ref, o_ref, s_scr, mask_scr, *, seq):
    tq = TQ_WIN
    n_t = q_ref.shape[1] // tq
    grp = A_HEADS // A_KV
    kb = tq + 2 * A_WINDOW
    tile0 = pl.program_id(1) * n_t
    last_tile = seq // tq - 1
    sinks = [jnp.concatenate([jnp.full((1, tq), sink_ref[0, kv * grp + g] * LOG2_E, F32)
                              for g in range(grp)], axis=1) for kv in range(A_KV)]

    @pl.when((pl.program_id(0) == 0) & (pl.program_id(1) == 0))
    def _():
        rel = (lax.broadcasted_iota(jnp.int32, (kb, tq), 0) - lax.broadcasted_iota(jnp.int32, (kb, tq), 1))
        for cls in range(3):
            ok = jnp.abs(rel - cls * A_WINDOW) <= A_WINDOW
            mask_scr[cls] = jnp.concatenate([jnp.where(ok, 0.0, NEG)] * grp, axis=1)

    def band(t):
        g = tile0 + t
        q0 = g * tq
        cls = jnp.where(g == 0, 0, jnp.where(g == last_tile, 2, 1))
        return cls, pl.multiple_of(jnp.clip(q0 - A_WINDOW, 0, seq - kb), A_WINDOW)

    def scores(slot, t):
        cls, ks = band(t)
        for kv in range(A_KV):
            q = q_ref[kv * grp:(kv + 1) * grp, pl.ds(pl.multiple_of(t * tq, tq), tq), :]
            q = q.reshape(grp * tq, HEAD_DIM)
            s_scr[slot, kv, :kb, :] = _dot_nt(k_ref[kv, pl.ds(ks, kb), :], q) + mask_scr[cls]
            s_scr[slot, kv, kb:, :] = _dot_nt(kc_ref[kv], q)

    def absorb(slot, t):
        _, ks = band(t)
        accs, sink_w = [], []
        for kv in range(A_KV):
            s = s_scr[slot, kv]
            m = jnp.maximum(jnp.max(s, axis=0, keepdims=True), sinks[kv])
            e = jnp.exp2(s - m).astype(BF16)
            accs.append(_dot(vt_ref[kv, :, pl.ds(ks, kb)], e[:kb]) + _dot(vtc_ref[kv], e[kb:]))
            sink_w.append(jnp.exp2(sinks[kv] - m))
        return jnp.concatenate(accs, axis=0), jnp.concatenate(sink_w, axis=0)

    def store(t, stage):
        acc, sink_w = stage
        outs = []
        for kv in range(A_KV):
            a = acc[kv * VT_ROWS:(kv + 1) * VT_ROWS]
            o = a[:HEAD_DIM] / (a[HEAD_DIM:HEAD_DIM + 1] + sink_w[kv:kv + 1])
            outs.extend(o[:, g * tq:(g + 1) * tq] for g in range(grp))
        o_ref[pl.ds(pl.multiple_of(t * tq, tq), tq), :] = jnp.concatenate(outs, axis=0).T.astype(o_ref.dtype)

    scores(0, 0)

    def body(j, prev):
        for u in range(WIN_UNROLL):
            t = WIN_UNROLL * j + u
            scores((u + 1) % 2, jnp.minimum(t + 1, n_t - 1))
            cur = absorb(u % 2, t)
            store(jnp.maximum(t - 1, 0), prev)
            prev = cur
        return prev

    init = (jnp.ones((A_KV * VT_ROWS, grp * tq), F32), jnp.zeros((A_KV, grp * tq), F32))
    store(n_t - 1, lax.fori_loop(0, n_t // WIN_UNROLL, body, init))


def _attn_window(q, k, vt, sink, *, n_batch, seq, ctx_len):
    r = n_batch * seq
    n_t = min(WIN_TILES, seq // TQ_WIN)
    assert n_t % WIN_UNROLL == 0
    tb = n_t * TQ_WIN
    nq = seq // tb
    cb = n_batch * seq // ctx_len
    return pl.pallas_call(
        functools.partial(_win_kernel, seq=seq),
        grid=(n_batch, nq),
        in_specs=[
            pl.BlockSpec(memory_space=pltpu.SMEM),
            pl.BlockSpec((A_HEADS, tb, HEAD_DIM), lambda b, i: (0, b * nq + i, 0)),
            pl.BlockSpec((A_KV, seq, HEAD_DIM), lambda b, i: (0, b, 0)),
            pl.BlockSpec((A_KV, VT_ROWS, seq), lambda b, i: (0, 0, b)),
            pl.BlockSpec((A_KV, ctx_len, HEAD_DIM), lambda b, i: (0, cb + b, 0)),
            pl.BlockSpec((A_KV, VT_ROWS, ctx_len), lambda b, i: (0, 0, cb + b)),
        ],
        out_specs=pl.BlockSpec((tb, A_HEADS * HEAD_DIM), lambda b, i: (b * nq + i, 0)),
        out_shape=jax.ShapeDtypeStruct((r, A_HEADS * HEAD_DIM), BF16),
        scratch_shapes=[pltpu.VMEM((2, A_KV, TQ_WIN + 2 * A_WINDOW + ctx_len, (A_HEADS // A_KV) * TQ_WIN), F32),
                        pltpu.VMEM((3, TQ_WIN + 2 * A_WINDOW, (A_HEADS // A_KV) * TQ_WIN), F32)],
        compiler_params=_cparams(2),
        name="attn_window",
    )(sink.reshape(1, A_HEADS), q, k, vt, k, vt)


def _glob_kernel(bound_ref, q_ref, k_ref, vt_ref, kc_ref, vtc_ref, o_ref, s_scr, *, seq):
    tq = q_ref.shape[1]
    grp = B_HEADS // B_KV
    n_chunks = seq // TK_GLOB
    bound = bound_ref[0, 0]

    def queries(kv):
        return q_ref[kv * grp:(kv + 1) * grp].reshape(grp * tq, HEAD_DIM)

    def scores(c, kv):
        off = pl.multiple_of(c * TK_GLOB, TK_GLOB)
        return _dot_nt(k_ref[kv, pl.ds(off, TK_GLOB), :], queries(kv))

    def vt_chunk(c, kv):
        return vt_ref[kv, :, pl.ds(pl.multiple_of(c * TK_GLOB, TK_GLOB), TK_GLOB)]

    def finish(accs):
        outs = []
        for acc in accs:
            o = acc[:HEAD_DIM] / acc[HEAD_DIM:HEAD_DIM + 1]
            outs.extend(o[:, g * tq:(g + 1) * tq] for g in range(grp))
        o_ref[...] = jnp.concatenate(outs, axis=0).T.astype(o_ref.dtype)

    @pl.when(bound <= GLOB_SAFE_BOUND)
    def _():
        def weights(s):
            return jnp.exp2(s - bound).astype(BF16)

        acc0 = tuple(_dot(vtc_ref[kv], weights(_dot_nt(kc_ref[kv], queries(kv)))) for kv in range(B_KV))

        def body(c, accs):
            return tuple(acc + _dot(vt_chunk(c, kv), weights(scores(c, kv))) for kv, acc in enumerate(accs))

        finish(lax.fori_loop(0, n_chunks, body, acc0, unroll=8))

    @pl.when(bound > GLOB_SAFE_BOUND)
    def _():
        def absorb(s, vt, m, acc):
            m_new = jnp.maximum(m, jnp.max(s, axis=0, keepdims=True))
            e = jnp.exp2(s - m_new).astype(BF16)
            return m_new, jnp.exp2(m - m_new) * acc + _dot(vt, e)

        accs = []
        for kv in range(B_KV):
            s = _dot_nt(kc_ref[kv], queries(kv))
            m0 = jnp.max(s, axis=0, keepdims=True)
            a0 = _dot(vtc_ref[kv], jnp.exp2(s - m0).astype(BF16))
            s_scr[0] = scores(0, kv)

            def body(j, carry, kv=kv):
                m, acc = carry
                s_scr[1] = scores(2 * j + 1, kv)
                m, acc = absorb(s_scr[0], vt_chunk(2 * j, kv), m, acc)
                s_scr[0] = scores(jnp.minimum(2 * j + 2, n_chunks - 1), kv)
                return absorb(s_scr[1], vt_chunk(2 * j + 1, kv), m, acc)

            accs.append(lax.fori_loop(0, n_chunks // 2, body, (m0, a0))[1])
        finish(accs)


def _logit_bound(q_gain, k_gain):
    bound = (HEAD_DIM * QK_SCALE * LOG2_E * 1.02) * jnp.max(jnp.abs(q_gain)) * jnp.max(jnp.abs(k_gain))
    return bound.astype(F32).reshape(1, 1)


def _attn_global(q, k, vt, bound, *, n_batch, seq, ctx_len):
    r = n_batch * seq
    tq = TQ_GLOB
    nq = seq // tq
    cb = n_batch * seq // ctx_len
    return pl.pallas_call(
        functools.partial(_glob_kernel, seq=seq),
        grid=(n_batch, nq),
        in_specs=[
            pl.BlockSpec(memory_space=pltpu.SMEM),
            pl.BlockSpec((B_HEADS, tq, HEAD_DIM), lambda b, i: (0, b * nq + i, 0)),
            pl.BlockSpec((B_KV, seq, HEAD_DIM), lambda b, i: (0, b, 0)),
            pl.BlockSpec((B_KV, VT_ROWS, seq), lambda b, i: (0, 0, b)),
            pl.BlockSpec((B_KV, ctx_len, HEAD_DIM), lambda b, i: (0, cb + b, 0)),
            pl.BlockSpec((B_KV, VT_ROWS, ctx_len), lambda b, i: (0, 0, cb + b)),
        ],
        out_specs=pl.BlockSpec((tq, B_HEADS * HEAD_DIM), lambda b, i: (b * nq + i, 0)),
        out_shape=jax.ShapeDtypeStruct((r, B_HEADS * HEAD_DIM), BF16),
        scratch_shapes=[pltpu.VMEM((2, TK_GLOB, (B_HEADS // B_KV) * tq), F32)],
        compiler_params=_cparams(2),
        name="attn_global",
    )(bound, q, k, vt, k, vt)


def _nbr_kernel(q_ref, k_ref, vt_ref, kc_ref, vtc_ref, bias_ref, o_ref, s_scr, *, rows):
    tq = NBR_ROWS * GRID_W
    kk = NBR_KROWS * GRID_W
    n_t = q_ref.shape[1] // tq
    tile0 = pl.program_id(2) * n_t
    last_tile = rows // NBR_ROWS - 1

    def band(t):
        g = tile0 + t
        cls = jnp.where(g == 0, 0, jnp.where(g == last_tile, 2, 1))
        kr0 = jnp.clip(g * NBR_ROWS - NA_KH // 2, 0, rows - NBR_KROWS)
        return cls, pl.multiple_of(kr0 * GRID_W, LANES)

    def scores(slot, t):
        cls, ks = band(t)
        for h in range(2):
            q = q_ref[h, pl.ds(pl.multiple_of(t * tq, tq), tq), :]
            s_scr[slot, h, :kk, :] = _dot_nt(k_ref[h, pl.ds(ks, kk), :], q) + bias_ref[cls, h]
            s_scr[slot, h, kk:, :] = _dot_nt(kc_ref[h], q)

    def absorb(slot, t):
        _, ks = band(t)
        accs = []
        for h in range(2):
            s = s_scr[slot, h]
            e = jnp.exp2(s - jnp.max(s, axis=0, keepdims=True)).astype(BF16)
            accs.append(_dot(vt_ref[h, :, pl.ds(ks, kk)], e[:kk]) + _dot(vtc_ref[h], e[kk:]))
        return jnp.concatenate(accs, axis=0)

    def store(t, acc):
        o = [acc[h * VT_ROWS:h * VT_ROWS + HEAD_DIM] / acc[h * VT_ROWS + HEAD_DIM:h * VT_ROWS + HEAD_DIM + 1]
             for h in range(2)]
        o_ref[pl.ds(pl.multiple_of(t * tq, tq), tq), :] = jnp.concatenate(o, axis=0).T.astype(o_ref.dtype)

    scores(0, 0)

    def body(j, prev):
        for u in range(NBR_UNROLL):
            t = NBR_UNROLL * j + u
            scores((u + 1) % 2, jnp.minimum(t + 1, n_t - 1))
            cur = absorb(u % 2, t)
            store(jnp.maximum(t - 1, 0), prev)
            prev = cur
        return prev

    store(n_t - 1, lax.fori_loop(0, n_t // NBR_UNROLL, body, jnp.ones((2 * VT_ROWS, tq), F32)))


def _nbr_bias_kernel(rpb_ref, o_ref, *, rows):
    h = pl.program_id(0)
    n_dc = 2 * NA_KW - 1
    kcol = lax.broadcasted_iota(jnp.int32, (GRID_W, GRID_W), 0)
    col = lax.broadcasted_iota(jnp.int32, (GRID_W, GRID_W), 1)
    dc = kcol - col + NA_KW - 1
    cs = jnp.clip(col - NA_KW // 2, 0, GRID_W - NA_KW)
    col_ok = (kcol >= cs) & (kcol < cs + NA_KW)
    neg = jnp.full((GRID_W, GRID_W), NEG, F32)
    by_dr = []
    for dr in range(2 * NA_KH - 1):
        blk = neg
        for j in range(n_dc):
            blk = jnp.where(dc == j, rpb_ref[h, dr * n_dc + j] * LOG2_E, blk)
        by_dr.append(jnp.where(col_ok, blk, NEG))
    for cls, r0 in enumerate((0, NBR_ROWS, rows - NBR_ROWS)):
        kr0 = min(max(r0 - NA_KH // 2, 0), rows - NBR_KROWS)
        for kri in range(NBR_KROWS):
            krow = kr0 + kri
            blocks = []
            for ri in range(NBR_ROWS):
                rs = min(max(r0 + ri - NA_KH // 2, 0), rows - NA_KH)
                blocks.append(by_dr[krow - r0 - ri + NA_KH - 1] if rs <= krow < rs + NA_KH else neg)
            o_ref[cls, 0, kri * GRID_W:(kri + 1) * GRID_W, :] = jnp.concatenate(blocks, axis=-1)


def _nbr_bias(rpb, rows):
    n_heads = rpb.shape[0]
    tq, kk = NBR_ROWS * GRID_W, NBR_KROWS * GRID_W
    return pl.pallas_call(
        functools.partial(_nbr_bias_kernel, rows=rows),
        grid=(n_heads,),
        in_specs=[pl.BlockSpec(memory_space=pltpu.SMEM)],
        out_specs=pl.BlockSpec((3, 1, kk, tq), lambda h: (0, h, 0, 0)),
        out_shape=jax.ShapeDtypeStruct((3, n_heads, kk, tq), F32),
        compiler_params=_cparams(1),
        name="nbr_bias",
    )(rpb.astype(F32).reshape(n_heads, -1))


def _attn_nbr(q, k, vt, bias, *, n_batch, seq, ctx_len):
    r = n_batch * seq
    rows = seq // GRID_W
    tq = NBR_ROWS * GRID_W
    kk = NBR_KROWS * GRID_W
    n_t = min(NBR_TILES, seq // tq)
    assert n_t % NBR_UNROLL == 0 and seq % (n_t * tq) == 0 and rows >= NBR_KROWS
    tb = n_t * tq
    nq = seq // tb
    cb = n_batch * seq // ctx_len
    return pl.pallas_call(
        functools.partial(_nbr_kernel, rows=rows),
        grid=(n_batch, C_HEADS // 2, nq),
        in_specs=[
            pl.BlockSpec((2, tb, HEAD_DIM), lambda b, hp, i: (hp, b * nq + i, 0)),
            pl.BlockSpec((2, seq, HEAD_DIM), lambda b, hp, i: (hp, b, 0)),
            pl.BlockSpec((2, VT_ROWS, seq), lambda b, hp, i: (hp, 0, b)),
            pl.BlockSpec((2, ctx_len, HEAD_DIM), lambda b, hp, i: (hp, cb + b, 0)),
            pl.BlockSpec((2, VT_ROWS, ctx_len), lambda b, hp, i: (hp, 0, cb + b)),
            pl.BlockSpec((3, 2, kk, tq), lambda b, hp, i: (0, hp, 0, 0)),
        ],
        out_specs=pl.BlockSpec((tb, 2 * HEAD_DIM), lambda b, hp, i: (b * nq + i, hp)),
        out_shape=jax.ShapeDtypeStruct((r, C_HEADS * HEAD_DIM), BF16),
        scratch_shapes=[pltpu.VMEM((2, 2, kk + ctx_len, tq), F32)],
        compiler_params=_cparams(3),
        name="attn_nbr",
    )(q, k, vt, k, vt, bias)


def _ctx_kernel(sink_ref, qw_ref, qg_ref, qn_ref, kw_ref, kg_ref, kn_ref, vw_ref, vg_ref, vn_ref,
                yw_o, yg_o, yn_o):
    def group(q_ref, k_ref, vt_ref, o_ref, n_heads, n_kv, sink=False):
        outs = []
        for h in range(n_heads):
            kv = h // (n_heads // n_kv)
            s = _dot_nt(k_ref[kv], q_ref[h])
            outs.append(_softmax_pv_t([(s, vt_ref[kv])],
                                      extra_logit=sink_ref[0, h] * LOG2_E if sink else None))
        o_ref[...] = jnp.concatenate(outs, axis=0).T.astype(o_ref.dtype)

    group(qw_ref, kw_ref, vw_ref, yw_o, A_HEADS, A_KV, sink=True)
    group(qg_ref, kg_ref, vg_ref, yg_o, B_HEADS, B_KV)
    group(qn_ref, kn_ref, vn_ref, yn_o, C_HEADS, C_HEADS)


def _attn_ctx(sink, qs, ks, vs, *, n_batch, seq, ctx_len):
    cb = n_batch * seq // ctx_len

    def hspec(a):
        if a.shape[1] == VT_ROWS:
            return pl.BlockSpec((a.shape[0], VT_ROWS, ctx_len), lambda b: (0, 0, cb + b))
        return pl.BlockSpec((a.shape[0], ctx_len, HEAD_DIM), lambda b: (0, cb + b, 0))

    widths = [q.shape[0] * HEAD_DIM for q in qs]
    return pl.pallas_call(
        _ctx_kernel,
        grid=(n_batch,),
        in_specs=[pl.BlockSpec(memory_space=pltpu.SMEM)] + [hspec(a) for a in (*qs, *ks, *vs)],
        out_specs=[pl.BlockSpec((ctx_len, w), lambda b: (b, 0)) for w in widths],
        out_shape=[jax.ShapeDtypeStruct((n_batch * ctx_len, w), BF16) for w in widths],
        compiler_params=_cparams(1),
        name="attn_ctx",
    )(sink.reshape(1, A_HEADS), *qs, *ks, *vs)


def _rope_tables(seq):
    t = jnp.arange(seq)
    row = (t // GRID_W).astype(F32)
    col = (t % GRID_W).astype(F32)
    n_freq = HEAD_DIM // 4
    inv = ROPE_THETA ** (-jnp.arange(n_freq, dtype=F32) / n_freq)
    ang = jnp.concatenate([row[:, None] * inv, col[:, None] * inv], axis=-1)
    cos, sin = jnp.cos(ang), jnp.sin(ang)
    cosf = jnp.concatenate([jnp.tile(cos, (1, 4)), jnp.ones((TM_QKV, LANES), F32)], axis=0)
    sinf = jnp.concatenate([jnp.tile(jnp.concatenate([-sin, sin], axis=-1), (1, 2)),
                            jnp.zeros((TM_QKV, LANES), F32)], axis=0)
    return cosf, sinf


def kernel(x, c, ctx, c_ctx, w_ada, b_ada, norm_ffn1, w_ffn1_gate, w_ffn1_up, w_ffn1_down,
           norm_mix, w_in, q_norm_glob, k_norm_glob, sink_win, rpb_nbr, w_out,
           norm_ffn2, w_ffn2_gate, w_ffn2_up, w_ffn2_down, norm_final):
    n_batch, seq, d = x.shape
    ctx_len = ctx.shape[1]
    depth = w_ada.shape[0]
    r_lat, r_ctx = n_batch * seq, n_batch * ctx_len
    assert seq % TM_QKV == 0 and r_ctx % TM_QKV == 0 and n_batch + 1 <= MOD_ROWS
    assert seq % (min(WIN_TILES, seq // TQ_WIN) * TQ_WIN) == 0
    assert seq % TQ_GLOB == 0 and seq % (2 * TK_GLOB) == 0
    assert seq % (NBR_ROWS * GRID_W) == 0 and r_lat % ctx_len == 0 and seq >= TQ_WIN + 2 * A_WINDOW
    n_lat, n_all, tpb = r_lat // TM, (r_lat + r_ctx) // TM, seq // TM
    tiles = dict(n_lat=n_lat, tpb=tpb, n_batch=n_batch)
    dims = dict(n_batch=n_batch, seq=seq, ctx_len=ctx_len)

    h_srcs = (x.reshape(r_lat, d), ctx.reshape(r_ctx, d))
    cc = jnp.zeros((MOD_ROWS, d), F32).at[:n_batch].set(c).at[n_batch].set(c_ctx)
    mod = _ada(cc, w_ada, b_ada).reshape(depth, MOD_ROWS, N_MOD, d)
    cosf, sinf = _rope_tables(seq)
    seg = jnp.asarray(np.kron(np.eye(2), np.full((HEAD_DIM, HEAD_DIM), 1.0 / HEAD_DIM)), BF16)

    ffn1 = [w.astype(BF16) for w in (w_ffn1_gate, w_ffn1_up, w_ffn1_down)]
    ffn2 = [w.astype(BF16) for w in (w_ffn2_gate, w_ffn2_up, w_ffn2_down)]
    w_in_b, w_out_b = w_in.astype(BF16), w_out.astype(BF16)

    for l in range(depth):
        last = l == depth - 1
        h = _ffn(h_srcs, mod, norm_ffn1, *ffn1, norm_final, layer=l, k0=MOD_FFN1, n_tiles=n_all, final=False,
                 **tiles)
        qw, qg, qn, kw, kg, kn, vw, vg, vn = _qkv(
            h, mod, norm_mix, w_in_b, q_norm_glob[l], k_norm_glob[l],
            cosf, sinf, seg, layer=l, n_batch=n_batch, seq=seq)
        ys = [(_attn_window(qw, kw, vw, sink_win[l], **dims),),
              (_attn_global(qg, kg, vg, _logit_bound(q_norm_glob[l], k_norm_glob[l]), **dims),),
              (_attn_nbr(qn, kn, vn, _nbr_bias(rpb_nbr[l], seq // GRID_W), **dims),)]
        if not last:
            ycs = _attn_ctx(sink_win[l], (qw, qg, qn), (kw, kg, kn), (vw, vg, vn), **dims)
            ys = [(y, yc) for (y,), yc in zip(ys, ycs)]
        h = _ffn((h,), mod, norm_ffn2, *ffn2, norm_final, layer=l, k0=MOD_FFN2,
                 n_tiles=n_lat if last else n_all,
                 final=last, y_srcs=ys, w_out=w_out_b, **tiles)
        h_srcs = (h,)
    return h.reshape(n_batch, seq, d)
```
